```python
import math
import jax, jax.numpy as jnp
from jax import lax
import numpy as np

D_MODEL = 4096
BATCH = 2
SEQ = 8192
DEPTH = 2

HEAD_DIM = 128
A_WIDTH = D_MODEL // 4
A_HEADS = A_WIDTH // HEAD_DIM
A_BRANCHES = ((128, 1), (512, 4), (2048, 16))
B_WIDTH = D_MODEL // 4
CONV_WIDTH = 3
C_WIDTH = D_MODEL // 2
C_HEADS = C_WIDTH // HEAD_DIM
C_KV_HEADS = C_HEADS // 4
C_KV_WIDTH = C_KV_HEADS * HEAD_DIM
Q_BLOCK = 128
ROPE_THETA = 10000.0
GRID_W = 64
NUM_BUCKETS = 32
MAX_DISTANCE = 1024
D_FF = 4 * D_MODEL
EPS = 1e-6
NEG = -1e30

MIX_WIDTH = A_WIDTH + B_WIDTH + C_WIDTH
IN_SIZES = (A_WIDTH, A_WIDTH, A_WIDTH, B_WIDTH, B_WIDTH, B_WIDTH, C_WIDTH, C_KV_WIDTH, C_KV_WIDTH)
IN_WIDTH = sum(IN_SIZES)
IN_SPLITS = tuple(int(v) for v in np.cumsum(IN_SIZES)[:-1])

kernel_name = 'hybrid_parallel_mixer_encoder'


def rms_norm(x, g):
    xf = x.astype(jnp.float32)
    y = xf * lax.rsqrt(jnp.mean(xf * xf, axis=-1, keepdims=True) + EPS)
    return (y * g.astype(jnp.float32)).astype(x.dtype)


def t5_bucket(rel):
    half = NUM_BUCKETS // 2
    max_exact = half // 2
    n = jnp.abs(rel)
    base = jnp.where(rel > 0, half, 0)
    nf = jnp.maximum(n, 1).astype(jnp.float32)
    large = max_exact + (jnp.log(nf / max_exact) / math.log(MAX_DISTANCE / max_exact) * (half - max_exact)).astype(jnp.int32)
    large = jnp.minimum(large, half - 1)
    return base + jnp.where(n < max_exact, n, large)


def dilated_branch(q, k, v, rel_bias, window, dilation):
    b, h, s, dh = q.shape
    span = window // (2 * dilation)
    L = s // dilation
    nb = -(-L // span)
    pad = nb * span - L

    def fold(t):
        return t.reshape(b, h, L, dilation, dh).transpose(0, 1, 3, 2, 4)

    qf = jnp.pad(fold(q), ((0, 0), (0, 0), (0, 0), (0, pad), (0, 0)))
    kf = jnp.pad(fold(k), ((0, 0), (0, 0), (0, 0), (span, pad + span), (0, 0)))
    vf = jnp.pad(fold(v), ((0, 0), (0, 0), (0, 0), (span, pad + span), (0, 0)))
    qb = qf.reshape(b, h, dilation, nb, span, dh)

    def windows(t):
        tb = t.reshape(b, h, dilation, nb + 2, span, dh)
        return jnp.concatenate([tb[:, :, :, :-2], tb[:, :, :, 1:-1], tb[:, :, :, 2:]], axis=4)

    kw, vw = windows(kf), windows(vf)
    logits = jnp.einsum('bhrnqd,bhrnkd->bhrnqk', qb, kw).astype(jnp.float32)
    qi = jnp.arange(span)
    ki = jnp.arange(3 * span)
    rel = ki[None, :] - span - qi[:, None]
    bias = rel_bias.astype(jnp.float32)[t5_bucket(rel * dilation)]
    bias = jnp.transpose(bias, (2, 0, 1))
    key_pos = jnp.arange(nb)[:, None] * span + ki[None, :] - span
    valid = (jnp.abs(rel)[None] <= span) & (key_pos[:, None, :] >= 0) & (key_pos[:, None, :] < L)
    logits = jnp.where(valid[None, None, None], logits + bias[None, :, None, None], NEG)
    m = jnp.max(logits, axis=-1, keepdims=True)
    p = jnp.exp(logits - m)
    l = jnp.sum(p, axis=-1, keepdims=True)
    o = jnp.einsum('bhrnqk,bhrnkd->bhrnqd', p, vw.astype(jnp.float32)) / l
    lse = (m + jnp.log(l))[..., 0]
    o = o.reshape(b, h, dilation, nb * span, dh)[:, :, :, :L].transpose(0, 1, 3, 2, 4).reshape(b, h, s, dh)
    lse = lse.reshape(b, h, dilation, nb * span)[:, :, :, :L].transpose(0, 1, 3, 2).reshape(b, h, s)
    return o, lse


def dilated_mixer(q, k, v, rel_bias):
    b, s = q.shape[:2]
    qt = q.transpose(0, 2, 1, 3) * (HEAD_DIM ** -0.5)
    kt = k.transpose(0, 2, 1, 3)
    vt = v.transpose(0, 2, 1, 3)
    outs, lses = [], []
    for window, dilation in A_BRANCHES:
        o, lse = dilated_branch(qt, kt, vt, rel_bias, window, dilation)
        outs.append(o)
        lses.append(lse)
    wts = jax.nn.softmax(jnp.stack(lses), axis=0)
    o = jnp.sum(wts[..., None] * jnp.stack(outs), axis=0)
    return o.transpose(0, 2, 1, 3).reshape(b, s, A_WIDTH).astype(q.dtype)


def short_conv_mixer(gate_b, gate_c, h, conv_w):
    u = gate_c * h
    y = lax.conv_general_dilated(u, conv_w[:, None, :].astype(u.dtype), window_strides=(1,),
                                 padding=((CONV_WIDTH // 2, CONV_WIDTH // 2),),
                                 dimension_numbers=('NWC', 'WIO', 'NWC'),
                                 feature_group_count=u.shape[-1])
    return gate_b * y


def rope_1d(xh, pos):
    quarter = xh.shape[-1] // 2
    inv = ROPE_THETA ** (-jnp.arange(quarter, dtype=jnp.float32) / quarter)
    ang = pos.astype(jnp.float32)[:, None] * inv[None, :]
    c = jnp.cos(ang)[None, :, None, :]
    sn = jnp.sin(ang)[None, :, None, :]
    x1 = xh[..., :quarter].astype(jnp.float32)
    x2 = xh[..., quarter:].astype(jnp.float32)
    return jnp.concatenate([x1 * c - x2 * sn, x2 * c + x1 * sn], axis=-1)


def axial_rope(x, row, col):
    half = HEAD_DIM // 2
    return jnp.concatenate([rope_1d(x[..., :half], row), rope_1d(x[..., half:], col)], axis=-1).astype(x.dtype)


def axial_gqa_mixer(q, k, v, q_gain, k_gain):
    b, s = q.shape[:2]
    rows = s // GRID_W
    row = jnp.repeat(jnp.arange(rows), GRID_W)
    col = jnp.tile(jnp.arange(GRID_W), rows)
    q = axial_rope(rms_norm(q, q_gain), row, col) * (HEAD_DIM ** -0.5)
    k = axial_rope(rms_norm(k, k_gain), row, col)
    g = C_HEADS // C_KV_HEADS
    nb = s // Q_BLOCK
    qb = q.reshape(b, nb, Q_BLOCK, C_KV_HEADS, g, HEAD_DIM).transpose(1, 0, 3, 4, 2, 5)
    kt = k.transpose(0, 2, 1, 3)
    vt = v.transpose(0, 2, 1, 3)

    def block(qblk):
        logits = jnp.einsum('bkgqd,bksd->bkgqs', qblk, kt).astype(jnp.float32)
        p = jax.nn.softmax(logits, axis=-1)
        return jnp.einsum('bkgqs,bksd->bkgqd', p, vt.astype(jnp.float32))

    o = lax.map(block, qb)
    return o.transpose(1, 0, 4, 2, 3, 5).reshape(b, s, C_WIDTH).astype(q.dtype)


def setup_inputs(seed: int = 0) -> dict:
    key = jax.random.key(seed)
    ks = jax.random.split(key, 17)
    f32 = jnp.float32

    def gain(k, shape):
        return 1.0 + 0.02 * jax.random.normal(k, shape, f32)

    return {
        'x': jax.random.normal(ks[0], (BATCH, SEQ, D_MODEL), f32),
        'rel_bias': 0.2 * jax.random.normal(ks[1], (NUM_BUCKETS, A_HEADS), f32),
        'pre_mix_norm': gain(ks[2], (DEPTH, D_MODEL)),
        'w_in': jax.random.normal(ks[3], (DEPTH, D_MODEL, IN_WIDTH), f32) * D_MODEL ** -0.5,
        'conv_w': jax.random.normal(ks[4], (DEPTH, CONV_WIDTH, B_WIDTH), f32) * CONV_WIDTH ** -0.5,
        'q_norm': gain(ks[5], (DEPTH, HEAD_DIM)),
        'k_norm': gain(ks[6], (DEPTH, HEAD_DIM)),
        'out_norm_a': gain(ks[7], (DEPTH, A_WIDTH)),
        'out_norm_b': gain(ks[8], (DEPTH, B_WIDTH)),
        'out_norm_c': gain(ks[9], (DEPTH, C_WIDTH)),
        'w_out': jax.random.normal(ks[10], (DEPTH, MIX_WIDTH, D_MODEL), f32) * MIX_WIDTH ** -0.5,
        'post_mix_norm': gain(ks[11], (DEPTH, D_MODEL)),
        'pre_mlp_norm': gain(ks[12], (DEPTH, D_MODEL)),
        'w_up': jax.random.normal(ks[13], (DEPTH, D_MODEL, D_FF), f32) * D_MODEL ** -0.5,
        'w_down': jax.random.normal(ks[14], (DEPTH, D_FF, D_MODEL), f32) * D_FF ** -0.5,
        'post_mlp_norm': gain(ks[15], (DEPTH, D_MODEL)),
    }


def reference(x, rel_bias, pre_mix_norm, w_in, conv_w, q_norm, k_norm, out_norm_a, out_norm_b,
              out_norm_c, w_out, post_mix_norm, pre_mlp_norm, w_up, w_down, post_mlp_norm):
    b, s, _ = x.shape
    for i in range(DEPTH):
        hn = rms_norm(x, pre_mix_norm[i])
        proj = hn @ w_in[i]
        qa, ka, va, gb, gc, hb, qc, kc, vc = jnp.split(proj, IN_SPLITS, axis=-1)
        oa = dilated_mixer(qa.reshape(b, s, A_HEADS, HEAD_DIM), ka.reshape(b, s, A_HEADS, HEAD_DIM),
                           va.reshape(b, s, A_HEADS, HEAD_DIM), rel_bias)
        ob = short_conv_mixer(gb, gc, hb, conv_w[i])
        oc = axial_gqa_mixer(qc.reshape(b, s, C_HEADS, HEAD_DIM), kc.reshape(b, s, C_KV_HEADS, HEAD_DIM),
                             vc.reshape(b, s, C_KV_HEADS, HEAD_DIM), q_norm[i], k_norm[i])
        mix = jnp.concatenate([rms_norm(oa, out_norm_a[i]), rms_norm(ob, out_norm_b[i]),
                               rms_norm(oc, out_norm_c[i])], axis=-1)
        x = x + rms_norm(mix @ w_out[i], post_mix_norm[i])
        hn = rms_norm(x, pre_mlp_norm[i])
        ff = jnp.square(jax.nn.relu(hn @ w_up[i])) @ w_down[i]
        x = x + rms_norm(ff, post_mlp_norm[i])
    return x
```

```python
import functools
import math

import numpy as np
import jax
import jax.numpy as jnp
from jax import lax
from jax.experimental import pallas as pl
from jax.experimental.pallas import tpu as pltpu

HEAD_DIM = 128
A_BRANCHES = ((128, 1), (512, 4), (2048, 16))
SPAN = 64
Q_TILE_A = 2 * SPAN
K_TILE_A = 4 * SPAN
ROPE_THETA = 10000.0
GRID_W = 64
NUM_BUCKETS = 32
MAX_DISTANCE = 1024
CONV_WIDTH = 3
GQA_GROUP = 4
EPS = 1e-6
NEG = -1e30
Q_SCALE = HEAD_DIM ** -0.5

V7X_VMEM_LIMIT_BYTES = 60 * 1024 * 1024
BF16_SUBLANE_TILE = 16

_F32 = jnp.float32
_BF16 = jnp.bfloat16


def _pick_tile(n, prefs):
    for t in prefs:
        if n % t == 0:
            return t
    return n


def _params(sem):
    return pltpu.CompilerParams(dimension_semantics=sem, vmem_limit_bytes=V7X_VMEM_LIMIT_BYTES)


def _row_rms_inv(v):
    return lax.rsqrt(jnp.mean(v * v, axis=-1, keepdims=True) + EPS)


ROW_CHUNK = 16


def _for_row_chunks(nrows, fn):
    def step(c, carry):
        fn(pl.ds(pl.multiple_of(c * ROW_CHUNK, ROW_CHUNK), ROW_CHUNK))
        return carry

    lax.fori_loop(0, nrows // ROW_CHUNK, step, 0)


def _norm_rows_to(dst_ref, src_ref, g_ref):
    def fn(rows):
        v = src_ref[rows, :].astype(_F32)
        dst_ref[rows, :] = (v * _row_rms_inv(v) * g_ref[...]).astype(dst_ref.dtype)

    _for_row_chunks(src_ref.shape[0], fn)


def _residual_norm_rows(o_ref, x_ref, g_ref):
    def fn(rows):
        y = o_ref[rows, :]
        o_ref[rows, :] = x_ref[rows, :] + y * _row_rms_inv(y) * g_ref[...]

    _for_row_chunks(o_ref.shape[0], fn)


def _in_proj_body(x_ref, g_ref, w_ref, oa_ref, or_ref, xn_ref, *, n_a):
    j = pl.program_id(1)

    @pl.when(j == 0)
    def _():
        _norm_rows_to(xn_ref, x_ref, g_ref)

    @pl.when(j < n_a)
    def _():
        oa_ref[...] = jnp.dot(xn_ref[...], w_ref[...], preferred_element_type=_F32)

    @pl.when(j >= n_a)
    def _():
        or_ref[...] = jnp.dot(xn_ref[...], w_ref[...], preferred_element_type=_F32).astype(_BF16)


def _in_proj(x, g, w, a_cols):
    m, d = x.shape
    n = w.shape[1]
    tm = _pick_tile(m, (512, 256, 128))
    tn = next(t for t in (1024, 512, 256, 128) if a_cols % t == 0 and (n - a_cols) % t == 0)
    n_a = a_cols // tn
    return pl.pallas_call(
        functools.partial(_in_proj_body, n_a=n_a),
        grid=(m // tm, n // tn),
        in_specs=[
            pl.BlockSpec((tm, d), lambda i, j: (i, 0)),
            pl.BlockSpec((1, d), lambda i, j: (0, 0)),
            pl.BlockSpec((d, tn), lambda i, j: (0, j)),
        ],
        out_specs=[
            pl.BlockSpec((tm, tn), lambda i, j: (i, jnp.minimum(j, n_a - 1))),
            pl.BlockSpec((tm, tn), lambda i, j: (i, jnp.maximum(j - n_a, 0))),
        ],
        out_shape=[
            jax.ShapeDtypeStruct((m, a_cols), _F32),
            jax.ShapeDtypeStruct((m, n - a_cols), _BF16),
        ],
        scratch_shapes=[pltpu.VMEM((tm, d), _BF16)],
        compiler_params=_params(("parallel", "arbitrary")),
        name="in_proj",
    )(x, g.reshape(1, d), w)


def _bucket_table():
    half = NUM_BUCKETS // 2
    max_exact = half // 2
    i = np.arange(Q_TILE_A)[:, None]
    c = np.arange(K_TILE_A)[None, :]
    out = np.zeros((len(A_BRANCHES), 3, Q_TILE_A, K_TILE_A), np.int32)
    for br, (_, dil) in enumerate(A_BRANCHES):
        for var in range(3):
            rel = c - SPAN * var - i
            dist = rel * dil
            n = np.abs(dist)
            t = np.log(np.maximum(n, 1) / max_exact) / math.log(MAX_DISTANCE / max_exact) * (half - max_exact)
            valid = np.abs(rel) <= SPAN
            frac = np.abs(t - np.round(t))
            assert np.all((frac > 1e-4) | (n <= max_exact) | (n >= MAX_DISTANCE) | ~valid), "bucket edge near an integer"
            large = np.minimum(max_exact + t.astype(np.int32), half - 1)
            bucket = np.where(dist > 0, half, 0) + np.where(n < max_exact, n, large)
            out[br, var] = np.where(valid, bucket, NUM_BUCKETS)
    return out


def _mixer_a_body(rel_ref, idx_ref, q_ref, k_ref, v_ref, o_ref, bias_ref, acc_ref, m_ref, l_ref, *, seq):
    h = pl.program_id(1)

    for br in range(len(A_BRANCHES)):
        for var in range(3):
            idx = idx_ref[br, var]

            def fill(bkt, bias, idx=idx):
                return jnp.where(idx == bkt, rel_ref[bkt, h], bias)

            bias_ref[br, var] = lax.fori_loop(0, NUM_BUCKETS, fill, jnp.full(idx.shape, NEG, _F32))

    for br, (_, dil) in enumerate(A_BRANCHES):
        length = seq // dil
        nblk = length // Q_TILE_A
        first = br == 0

        def rows(start, size, dil=dil):
            if dil == 1:
                return pl.ds(pl.multiple_of(start, SPAN), size)
            return pl.ds(start, size, stride=dil)

        def block(n, r, br=br, dil=dil, length=length, nblk=nblk, first=first, rows=rows):
            p0 = n * Q_TILE_A
            ks = jnp.clip(p0 - SPAN, 0, length - K_TILE_A)
            var = jnp.where(n == 0, 0, jnp.where(n == nblk - 1, 2, 1))
            qsl = rows(r + dil * p0, Q_TILE_A)
            ksl = rows(r + dil * ks, K_TILE_A)
            q = (q_ref[qsl, :] * Q_SCALE).astype(_BF16)
            k = k_ref[ksl, :].astype(_BF16)
            v = v_ref[ksl, :].astype(_BF16)
            s = lax.dot_general(q, k, (((1,), (1,)), ((), ())), preferred_element_type=_F32)
            s = s + bias_ref[br, var]
            m_blk = jnp.max(s, axis=-1, keepdims=True)
            if first:
                m_new = m_blk
                p = jnp.exp(s - m_new)
            else:
                m_old = m_ref[qsl, :]
                m_new = jnp.maximum(m_old, m_blk)
                p = jnp.exp(s - jnp.concatenate([m_new] * (K_TILE_A // HEAD_DIM), axis=-1))
            l_blk = jnp.sum(p, axis=-1, keepdims=True)
            pv = jnp.dot(p.astype(_BF16), v, preferred_element_type=_F32)
            if first:
                acc_ref[qsl, :] = pv
                l_ref[qsl, :] = jnp.broadcast_to(l_blk, (Q_TILE_A, HEAD_DIM))
                m_ref[qsl, :] = jnp.broadcast_to(m_new, (Q_TILE_A, HEAD_DIM))
            else:
                alpha = jnp.exp(m_old - m_new)
                acc_ref[qsl, :] = alpha * acc_ref[qsl, :] + pv
                l_ref[qsl, :] = alpha * l_ref[qsl, :] + l_blk
                m_ref[qsl, :] = m_new

        def residue(r, carry, block=block, nblk=nblk):
            def blk(n, c):
                block(n, r)
                return c

            return lax.fori_loop(0, nblk, blk, carry)

        lax.fori_loop(0, dil, residue, 0)

    chunk = _pick_tile(seq, (512, 256, 128))

    def finish(c, carry):
        sl = pl.ds(pl.multiple_of(c * chunk, chunk), chunk)
        o_ref[sl, :] = (acc_ref[sl, :] / l_ref[sl, :]).astype(o_ref.dtype)
        return carry

    lax.fori_loop(0, seq // chunk, finish, 0)


def _mixer_a(proj_a, rel_bias, idx_tbl, batch, seq, heads):
    m = proj_a.shape[0]
    assert seq % (A_BRANCHES[-1][1] * K_TILE_A) == 0, "sequence too short for the widest dilation"
    blk = (seq, HEAD_DIM)
    return pl.pallas_call(
        functools.partial(_mixer_a_body, seq=seq),
        grid=(batch, heads),
        in_specs=[
            pl.BlockSpec(memory_space=pltpu.SMEM),
            pl.BlockSpec(idx_tbl.shape, lambda b, h: (0, 0, 0, 0)),
            pl.BlockSpec(blk, lambda b, h: (b, h)),
            pl.BlockSpec(blk, lambda b, h: (b, heads + h)),
            pl.BlockSpec(blk, lambda b, h: (b, 2 * heads + h)),
        ],
        out_specs=pl.BlockSpec(blk, lambda b, h: (b, h)),
        out_shape=jax.ShapeDtypeStruct((m, heads * HEAD_DIM), _BF16),
        scratch_shapes=[
            pltpu.VMEM(idx_tbl.shape, _F32),
            pltpu.VMEM(blk, _F32),
            pltpu.VMEM(blk, _F32),
            pltpu.VMEM(blk, _F32),
        ],
        compiler_params=_params(("parallel", "parallel")),
        name="mixer_a",
    )(rel_bias, idx_tbl, proj_a, proj_a, proj_a)


def _mixer_b_body(gb_ref, gc_ref, hb_ref, gcp_ref, hbp_ref, gcn_ref, hbn_ref, w_ref, g_ref, o_ref, *, tiles_per_seq):
    i = pl.program_id(0)
    ts = gb_ref.shape[0]
    pos = i % tiles_per_seq
    u = gc_ref[...].astype(_F32) * hb_ref[...].astype(_F32)
    last = BF16_SUBLANE_TILE - 1
    u_prev = gcp_ref[last:last + 1, :].astype(_F32) * hbp_ref[last:last + 1, :].astype(_F32)
    u_next = gcn_ref[0:1, :].astype(_F32) * hbn_ref[0:1, :].astype(_F32)
    u_prev = jnp.where(pos == 0, 0.0, u_prev)
    u_next = jnp.where(pos == tiles_per_seq - 1, 0.0, u_next)
    row = lax.broadcasted_iota(jnp.int32, (ts, 1), 0)
    up = jnp.where(row == 0, u_prev, pltpu.roll(u, 1, 0))
    un = jnp.where(row == ts - 1, u_next, pltpu.roll(u, ts - 1, 0))
    y = gb_ref[...].astype(_F32) * (w_ref[0:1, :] * up + w_ref[1:2, :] * u + w_ref[2:3, :] * un)
    o_ref[...] = (y * _row_rms_inv(y) * g_ref[...]).astype(o_ref.dtype)


def _mixer_b(proj_r, conv_w, gain, seq):
    m = proj_r.shape[0]
    wb = gain.shape[0]
    ts = _pick_tile(seq, (512, 256, 128))
    halo = BF16_SUBLANE_TILE
    per = ts // halo
    nhalo = m // halo
    main = lambda c: pl.BlockSpec((ts, wb), lambda i: (i, c))
    prev = lambda c: pl.BlockSpec((halo, wb), lambda i: (jnp.maximum(i * per - 1, 0), c))
    nxt = lambda c: pl.BlockSpec((halo, wb), lambda i: (jnp.minimum((i + 1) * per, nhalo - 1), c))
    return pl.pallas_call(
        functools.partial(_mixer_b_body, tiles_per_seq=seq // ts),
        grid=(m // ts,),
        in_specs=[main(0), main(1), main(2), prev(1), prev(2), nxt(1), nxt(2),
                  pl.BlockSpec((CONV_WIDTH, wb), lambda i: (0, 0)),
                  pl.BlockSpec((1, wb), lambda i: (0, 0))],
        out_specs=pl.BlockSpec((ts, wb), lambda i: (i, 0)),
        out_shape=jax.ShapeDtypeStruct((m, wb), _BF16),
        compiler_params=_params(("parallel",)),
        name="mixer_b",
    )(proj_r, proj_r, proj_r, proj_r, proj_r, proj_r, proj_r, conv_w, gain.reshape(1, wb))


def _rope_tables(seq):
    quarter = HEAD_DIM // 4
    inv = ROPE_THETA ** (-jnp.arange(quarter, dtype=_F32) / quarter)
    t = jnp.arange(seq)
    ang_r = (t // GRID_W).astype(_F32)[:, None] * inv[None, :]
    ang_c = (t % GRID_W).astype(_F32)[:, None] * inv[None, :]
    zero = jnp.zeros_like(ang_r)
    cos = jnp.concatenate([jnp.cos(ang_r), jnp.cos(ang_r), jnp.cos(ang_c), jnp.cos(ang_c)], axis=-1)
    sin_hi = jnp.concatenate([-jnp.sin(ang_r), zero, -jnp.sin(ang_c), zero], axis=-1)
    sin_lo = jnp.concatenate([zero, jnp.sin(ang_r), zero, jnp.sin(ang_c)], axis=-1)
    return cos, sin_hi, sin_lo


def _rope_body(x_ref, g_ref, cos_ref, shi_ref, slo_ref, o_ref, *, q_heads):
    hh = pl.program_id(1)
    quarter = HEAD_DIM // 4
    xv = x_ref[...].astype(_F32)
    gain = jnp.where(hh < q_heads, g_ref[0:1, :], g_ref[1:2, :])
    xn = xv * _row_rms_inv(xv) * gain
    rot = (xn * cos_ref[...]
           + pltpu.roll(xn, HEAD_DIM - quarter, 1) * shi_ref[...]
           + pltpu.roll(xn, quarter, 1) * slo_ref[...])
    scale = jnp.where(hh < q_heads, Q_SCALE, 1.0)
    o_ref[...] = (rot * scale).astype(o_ref.dtype)


def _rope_qk(proj_r, gains, tables, seq, col0, q_heads, kv_heads):
    m = proj_r.shape[0]
    ts = _pick_tile(seq, (2048, 1024, 512, 256, 128))
    per_seq = seq // ts
    blk0 = col0 // HEAD_DIM
    nh = q_heads + kv_heads
    tab = pl.BlockSpec((ts, HEAD_DIM), lambda i, hh: (i % per_seq, 0))
    return pl.pallas_call(
        functools.partial(_rope_body, q_heads=q_heads),
        grid=(m // ts, nh),
        in_specs=[pl.BlockSpec((ts, HEAD_DIM), lambda i, hh: (i, blk0 + hh)),
                  pl.BlockSpec((2, HEAD_DIM), lambda i, hh: (0, 0)),
                  tab, tab, tab],
        out_specs=pl.BlockSpec((ts, HEAD_DIM), lambda i, hh: (i, hh)),
        out_shape=jax.ShapeDtypeStruct((m, nh * HEAD_DIM), _BF16),
        compiler_params=_params(("parallel", "parallel")),
        name="rope_qk",
    )(proj_r, gains, *tables)


def _attn_c_body(q_ref, k_ref, v_ref, o_ref, m_ref, l_ref, acc_ref, *, tk):
    seq = k_ref.shape[0]
    tq = q_ref.shape[0]
    m_ref[...] = jnp.full(m_ref.shape, NEG, _F32)
    l_ref[...] = jnp.zeros(l_ref.shape, _F32)
    acc_ref[...] = jnp.zeros(acc_ref.shape, _F32)

    def step(c, carry):
        off = pl.multiple_of(c * tk, tk)
        k = k_ref[pl.ds(off, tk), :]
        v = v_ref[pl.ds(off, tk), :]
        for g in range(GQA_GROUP):
            q = q_ref[:, g * HEAD_DIM:(g + 1) * HEAD_DIM]
            s = lax.dot_general(q, k, (((1,), (1,)), ((), ())), preferred_element_type=_F32)
            m_old = m_ref[g]
            m_new = jnp.maximum(m_old, jnp.max(s, axis=-1, keepdims=True))
            alpha = jnp.exp(m_old - m_new)
            p = jnp.exp(s - m_new)
            l_ref[g] = alpha * l_ref[g] + jnp.sum(p, axis=-1, keepdims=True)
            acc_ref[g] = alpha * acc_ref[g] + jnp.dot(p.astype(_BF16), v, preferred_element_type=_F32)
            m_ref[g] = m_new
        return carry

    lax.fori_loop(0, seq // tk, step, 0)
    for g in range(GQA_GROUP):
        o_ref[:, g * HEAD_DIM:(g + 1) * HEAD_DIM] = (acc_ref[g] / l_ref[g]).astype(o_ref.dtype)


def _attn_c(qk, proj_r, batch, seq, q_heads, kv_heads, v_col0):
    m = qk.shape[0]
    assert q_heads == GQA_GROUP * kv_heads
    tq = _pick_tile(seq, (256, 128))
    tk = _pick_tile(seq, (512, 256, 128))
    nq = seq // tq
    gw = GQA_GROUP * HEAD_DIM
    v_blk0 = v_col0 // HEAD_DIM
    return pl.pallas_call(
        functools.partial(_attn_c_body, tk=tk),
        grid=(batch, kv_heads, nq),
        in_specs=[pl.BlockSpec((tq, gw), lambda b, g, i: (b * nq + i, g)),
                  pl.BlockSpec((seq, HEAD_DIM), lambda b, g, i: (b, q_heads + g)),
                  pl.BlockSpec((seq, HEAD_DIM), lambda b, g, i: (b, v_blk0 + g))],
        out_specs=pl.BlockSpec((tq, gw), lambda b, g, i: (b * nq + i, g)),
        out_shape=jax.ShapeDtypeStruct((m, q_heads * HEAD_DIM), _BF16),
        scratch_shapes=[pltpu.VMEM((GQA_GROUP, tq, 1), _F32),
                        pltpu.VMEM((GQA_GROUP, tq, 1), _F32),
                        pltpu.VMEM((GQA_GROUP, tq, HEAD_DIM), _F32)],
        compiler_params=_params(("parallel", "parallel", "arbitrary")),
        name="attn_c",
    )(qk, qk, proj_r)


def _out_proj_body(oa_ref, ob_ref, oc_ref, ga_ref, gc_ref, w_ref, x_ref, gp_ref, o_ref, xn_ref, *, tk):
    k = pl.program_id(1)
    nk = pl.num_programs(1)

    @pl.when(k == 0)
    def _():
        def fn(rows):
            c0 = 0
            for ref, g_ref in ((oa_ref, ga_ref), (ob_ref, None), (oc_ref, gc_ref)):
                width = ref.shape[1]
                if g_ref is None:
                    val = ref[rows, :]
                else:
                    vf = ref[rows, :].astype(_F32)
                    val = (vf * _row_rms_inv(vf) * g_ref[...]).astype(_BF16)
                for c in range(width // tk):
                    xn_ref[c0 + c, rows, :] = val[:, c * tk:(c + 1) * tk]
                c0 += width // tk

        _for_row_chunks(oa_ref.shape[0], fn)
        o_ref[...] = jnp.dot(xn_ref[0], w_ref[...], preferred_element_type=_F32)

    @pl.when(k > 0)
    def _():
        o_ref[...] += jnp.dot(xn_ref[k], w_ref[...], preferred_element_type=_F32)

    @pl.when(k == nk - 1)
    def _():
        _residual_norm_rows(o_ref, x_ref, gp_ref)


def _out_proj(oa, ob, oc, ga, gc, w, x, gp):
    m, d = x.shape
    kdim = w.shape[0]
    wa, wb, wc = oa.shape[1], ob.shape[1], oc.shape[1]
    tm = _pick_tile(m, (512, 256, 128))
    tk = next(t for t in (512, 256, 128) if wa % t == 0 and wb % t == 0 and wc % t == 0)
    nk = kdim // tk
    row = lambda width: pl.BlockSpec((tm, width), lambda i, k: (i, 0))
    vec = lambda width: pl.BlockSpec((1, width), lambda i, k: (0, 0))
    return pl.pallas_call(
        functools.partial(_out_proj_body, tk=tk),
        grid=(m // tm, nk),
        in_specs=[row(wa), row(wb), row(wc), vec(wa), vec(wc),
                  pl.BlockSpec((tk, d), lambda i, k: (k, 0)),
                  row(d), vec(d)],
        out_specs=row(d),
        out_shape=jax.ShapeDtypeStruct((m, d), _F32),
        scratch_shapes=[pltpu.VMEM((nk, tm, tk), _BF16)],
        compiler_params=_params(("parallel", "arbitrary")),
        name="out_proj",
    )(oa, ob, oc, ga.reshape(1, wa), gc.reshape(1, wc), w, x, gp.reshape(1, d))


def _mlp_body(x_ref, g1_ref, wu_ref, wd_ref, g2_ref, o_ref, xn_ref, hid_ref):
    f = pl.program_id(1)
    nf = pl.num_programs(1)

    @pl.when(f == 0)
    def _():
        _norm_rows_to(xn_ref, x_ref, g1_ref)

    hid = jnp.dot(xn_ref[...], wu_ref[...], preferred_element_type=_F32)
    hid_ref[...] = jnp.square(jnp.maximum(hid, 0.0)).astype(_BF16)

    @pl.when(f == 0)
    def _():
        o_ref[...] = jnp.dot(hid_ref[...], wd_ref[...], preferred_element_type=_F32)

    @pl.when(f > 0)
    def _():
        o_ref[...] += jnp.dot(hid_ref[...], wd_ref[...], preferred_element_type=_F32)

    @pl.when(f == nf - 1)
    def _():
        _residual_norm_rows(o_ref, x_ref, g2_ref)


def _mlp(x, g1, wu, wd, g2):
    m, d = x.shape
    dff = wu.shape[1]
    tm = _pick_tile(m, (512, 256, 128))
    tf = _pick_tile(dff, (512, 256, 128))
    row = pl.BlockSpec((tm, d), lambda i, f: (i, 0))
    vec = pl.BlockSpec((1, d), lambda i, f: (0, 0))
    return pl.pallas_call(
        _mlp_body,
        grid=(m // tm, dff // tf),
        in_specs=[row, vec,
                  pl.BlockSpec((d, tf), lambda i, f: (0, f)),
                  pl.BlockSpec((tf, d), lambda i, f: (f, 0)),
                  vec],
        out_specs=row,
        out_shape=jax.ShapeDtypeStruct((m, d), _F32),
        scratch_shapes=[pltpu.VMEM((tm, d), _BF16), pltpu.VMEM((tm, tf), _BF16)],
        compiler_params=_params(("parallel", "arbitrary")),
        name="mlp",
    )(x, g1.reshape(1, d), wu, wd, g2.reshape(1, d))


def kernel(x, rel_bias, pre_mix_norm, w_in, conv_w, q_norm, k_norm, out_norm_a, out_norm_b, out_norm_c,
           w_out, post_mix_norm, pre_mlp_norm, w_up, w_down, post_mlp_norm):
    batch, seq, d = x.shape
    depth = w_in.shape[0]
    wa, wb, wc = out_norm_a.shape[1], out_norm_b.shape[1], out_norm_c.shape[1]
    in_width = w_in.shape[2]
    kv_width = (in_width - 3 * wa - 3 * wb - wc) // 2
    a_heads, q_heads, kv_heads = wa // HEAD_DIM, wc // HEAD_DIM, kv_width // HEAD_DIM
    assert all(w // (2 * dil) == SPAN for w, dil in A_BRANCHES)
    assert rel_bias.shape == (NUM_BUCKETS, a_heads) and seq % GRID_W == 0

    idx_tbl = jnp.asarray(_bucket_table())
    tables = _rope_tables(seq)
    xf = x.reshape(batch * seq, d)
    for i in range(depth):
        proj_a, proj_r = _in_proj(xf, pre_mix_norm[i], w_in[i].astype(_BF16), 3 * wa)
        oa = _mixer_a(proj_a, rel_bias, idx_tbl, batch, seq, a_heads)
        ob = _mixer_b(proj_r, conv_w[i], out_norm_b[i], seq)
        qk = _rope_qk(proj_r, jnp.stack([q_norm[i], k_norm[i]]), tables, seq, 3 * wb, q_heads, kv_heads)
        oc = _attn_c(qk, proj_r, batch, seq, q_heads, kv_heads, 3 * wb + wc + kv_width)
        xf = _out_proj(oa, ob, oc, out_norm_a[i], out_norm_c[i], w_out[i].astype(_BF16), xf, post_mix_norm[i])
        xf = _mlp(xf, pre_mlp_norm[i], w_up[i].astype(_BF16), w_down[i].astype(_BF16), post_mlp_norm[i])
    return xf.reshape(batch, seq, d)
```

```python
import functools
import math

import numpy as np
import jax
import jax.numpy as jnp
from jax import lax
from jax.experimental import pallas as pl
from jax.experimental.pallas import tpu as pltpu

HEAD_DIM = 128
A_BRANCHES = ((128, 1), (512, 4), (2048, 16))
SPAN = 64
Q_TILE_A = 2 * SPAN
K_TILE_A = 4 * SPAN
ROPE_THETA = 10000.0
GRID_W = 64
NUM_BUCKETS = 32
MAX_DISTANCE = 1024
CONV_WIDTH = 3
GQA_GROUP = 4
EPS = 1e-6
NEG = -1e30
Q_SCALE = HEAD_DIM ** -0.5
LOG2_E = math.log2(math.e)

V7X_VMEM_LIMIT_BYTES = 60 * 1024 * 1024
BF16_SUBLANE_TILE = 16

_F32 = jnp.float32
_BF16 = jnp.bfloat16


def _pick_tile(n, prefs):
    for t in prefs:
        if n % t == 0:
            return t
    return n


def _params(sem):
    return pltpu.CompilerParams(dimension_semantics=sem, vmem_limit_bytes=V7X_VMEM_LIMIT_BYTES)


def _row_rms_inv(v):
    return lax.rsqrt(jnp.mean(v * v, axis=-1, keepdims=True) + EPS)


ROW_CHUNK = 16


def _for_row_chunks(nrows, fn):
    def step(c, carry):
        fn(pl.ds(pl.multiple_of(c * ROW_CHUNK, ROW_CHUNK), ROW_CHUNK))
        return carry

    lax.fori_loop(0, nrows // ROW_CHUNK, step, 0)


def _norm_rows_to(dst_ref, src_ref, g_ref):
    def fn(rows):
        v = src_ref[rows, :].astype(_F32)
        dst_ref[rows, :] = (v * _row_rms_inv(v) * g_ref[...]).astype(dst_ref.dtype)

    _for_row_chunks(src_ref.shape[0], fn)


def _residual_norm_rows(o_ref, x_ref, g_ref):
    def fn(rows):
        y = o_ref[rows, :]
        o_ref[rows, :] = x_ref[rows, :] + y * _row_rms_inv(y) * g_ref[...]

    _for_row_chunks(o_ref.shape[0], fn)


def _in_proj_body(x_ref, g_ref, w_ref, oa_ref, or_ref, xn_ref, *, n_a):
    j = pl.program_id(1)

    @pl.when(j == 0)
    def _():
        _norm_rows_to(xn_ref, x_ref, g_ref)

    @pl.when(j < n_a)
    def _():
        oa_ref[...] = jnp.dot(xn_ref[...], w_ref[...], preferred_element_type=_F32)

    @pl.when(j >= n_a)
    def _():
        or_ref[...] = jnp.dot(xn_ref[...], w_ref[...], preferred_element_type=_F32).astype(_BF16)


def _in_proj(x, g, w, a_cols):
    m, d = x.shape
    n = w.shape[1]
    tm = _pick_tile(m, (512, 256, 128))
    tn = next(t for t in (1024, 512, 256, 128) if a_cols % t == 0 and (n - a_cols) % t == 0)
    n_a = a_cols // tn
    return pl.pallas_call(
        functools.partial(_in_proj_body, n_a=n_a),
        grid=(m // tm, n // tn),
        in_specs=[
            pl.BlockSpec((tm, d), lambda i, j: (i, 0)),
            pl.BlockSpec((1, d), lambda i, j: (0, 0)),
            pl.BlockSpec((d, tn), lambda i, j: (0, j)),
        ],
        out_specs=[
            pl.BlockSpec((tm, tn), lambda i, j: (i, jnp.minimum(j, n_a - 1))),
            pl.BlockSpec((tm, tn), lambda i, j: (i, jnp.maximum(j - n_a, 0))),
        ],
        out_shape=[
            jax.ShapeDtypeStruct((m, a_cols), _F32),
            jax.ShapeDtypeStruct((m, n - a_cols), _BF16),
        ],
        scratch_shapes=[pltpu.VMEM((tm, d), _BF16)],
        compiler_params=_params(("parallel", "arbitrary")),
        name="in_proj",
    )(x, g.reshape(1, d), w)


def _bucket_table():
    half = NUM_BUCKETS // 2
    max_exact = half // 2
    i = np.arange(Q_TILE_A)[:, None]
    c = np.arange(K_TILE_A)[None, :]
    out = np.zeros((len(A_BRANCHES), 3, Q_TILE_A, K_TILE_A), np.int32)
    for br, (_, dil) in enumerate(A_BRANCHES):
        for var in range(3):
            rel = c - SPAN * var - i
            dist = rel * dil
            n = np.abs(dist)
            t = np.log(np.maximum(n, 1) / max_exact) / math.log(MAX_DISTANCE / max_exact) * (half - max_exact)
            valid = np.abs(rel) <= SPAN
            frac = np.abs(t - np.round(t))
            assert np.all((frac > 1e-4) | (n <= max_exact) | (n >= MAX_DISTANCE) | ~valid), "bucket edge near an integer"
            large = np.minimum(max_exact + t.astype(np.int32), half - 1)
            bucket = np.where(dist > 0, half, 0) + np.where(n < max_exact, n, large)
            out[br, var] = np.where(valid, bucket, NUM_BUCKETS)
    return out


def _mixer_a_body(rel_ref, idx_ref, q_ref, k_ref, v_ref, o_ref, bias_ref, acc_ref, m_ref, l_ref, *, seq):
    h = pl.program_id(1)

    for br in range(len(A_BRANCHES)):
        for var in range(3):
            idx = idx_ref[br, var]

            def fill(bkt, bias, idx=idx):
                return jnp.where(idx == bkt, rel_ref[bkt, h], bias)

            bias_ref[br, var] = lax.fori_loop(0, NUM_BUCKETS, fill, jnp.full(idx.shape, NEG, _F32))

    for br, (_, dil) in enumerate(A_BRANCHES):
        length = seq // dil
        nblk = length // Q_TILE_A
        first = br == 0

        def rows(start, size, dil=dil):
            if dil == 1:
                return pl.ds(pl.multiple_of(start, SPAN), size)
            return pl.ds(start, size, stride=dil)

        def block(n, r, br=br, dil=dil, length=length, nblk=nblk, first=first, rows=rows):
            p0 = n * Q_TILE_A
            ks = jnp.clip(p0 - SPAN, 0, length - K_TILE_A)
            var = jnp.where(n == 0, 0, jnp.where(n == nblk - 1, 2, 1))
            qsl = rows(r + dil * p0, Q_TILE_A)
            ksl = rows(r + dil * ks, K_TILE_A)
            q = (q_ref[qsl, :] * Q_SCALE).astype(_BF16)
            k = k_ref[ksl, :].astype(_BF16)
            v = v_ref[ksl, :].astype(_BF16)
            s = lax.dot_general(q, k, (((1,), (1,)), ((), ())), preferred_element_type=_F32)
            s = s + bias_ref[br, var]
            m_blk = jnp.max(s, axis=-1, keepdims=True)
            if first:
                m_new = m_blk
                p = jnp.exp(s - m_new)
            else:
                m_old = m_ref[qsl, :]
                m_new = jnp.maximum(m_old, m_blk)
                p = jnp.exp(s - jnp.concatenate([m_new] * (K_TILE_A // HEAD_DIM), axis=-1))
            l_blk = jnp.sum(p, axis=-1, keepdims=True)
            pv = jnp.dot(p.astype(_BF16), v, preferred_element_type=_F32)
            if first:
                acc_ref[qsl, :] = pv
                l_ref[qsl, :] = jnp.broadcast_to(l_blk, (Q_TILE_A, HEAD_DIM))
                m_ref[qsl, :] = jnp.broadcast_to(m_new, (Q_TILE_A, HEAD_DIM))
            else:
                alpha = jnp.exp(m_old - m_new)
                acc_ref[qsl, :] = alpha * acc_ref[qsl, :] + pv
                l_ref[qsl, :] = alpha * l_ref[qsl, :] + l_blk
                m_ref[qsl, :] = m_new

        def residue(r, carry, block=block, nblk=nblk):
            def blk(n, c):
                block(n, r)
                return c

            return lax.fori_loop(0, nblk, blk, carry)

        lax.fori_loop(0, dil, residue, 0)

    chunk = _pick_tile(seq, (512, 256, 128))

    def finish(c, carry):
        sl = pl.ds(pl.multiple_of(c * chunk, chunk), chunk)
        o_ref[sl, :] = (acc_ref[sl, :] / l_ref[sl, :]).astype(o_ref.dtype)
        return carry

    lax.fori_loop(0, seq // chunk, finish, 0)


def _mixer_a(proj_a, rel_bias, idx_tbl, batch, seq, heads):
    m = proj_a.shape[0]
    assert seq % (A_BRANCHES[-1][1] * K_TILE_A) == 0, "sequence too short for the widest dilation"
    blk = (seq, HEAD_DIM)
    return pl.pallas_call(
        functools.partial(_mixer_a_body, seq=seq),
        grid=(batch, heads),
        in_specs=[
            pl.BlockSpec(memory_space=pltpu.SMEM),
            pl.BlockSpec(idx_tbl.shape, lambda b, h: (0, 0, 0, 0)),
            pl.BlockSpec(blk, lambda b, h: (b, h)),
            pl.BlockSpec(blk, lambda b, h: (b, heads + h)),
            pl.BlockSpec(blk, lambda b, h: (b, 2 * heads + h)),
        ],
        out_specs=pl.BlockSpec(blk, lambda b, h: (b, h)),
        out_shape=jax.ShapeDtypeStruct((m, heads * HEAD_DIM), _BF16),
        scratch_shapes=[
            pltpu.VMEM(idx_tbl.shape, _F32),
            pltpu.VMEM(blk, _F32),
            pltpu.VMEM(blk, _F32),
            pltpu.VMEM(blk, _F32),
        ],
        compiler_params=_params(("parallel", "parallel")),
        name="mixer_a",
    )(rel_bias, idx_tbl, proj_a, proj_a, proj_a)


def _mixer_b_body(gb_ref, gc_ref, hb_ref, gcp_ref, hbp_ref, gcn_ref, hbn_ref, w_ref, g_ref, o_ref, *, tiles_per_seq):
    i = pl.program_id(0)
    ts = gb_ref.shape[0]
    pos = i % tiles_per_seq
    u = gc_ref[...].astype(_F32) * hb_ref[...].astype(_F32)
    last = BF16_SUBLANE_TILE - 1
    u_prev = gcp_ref[last:last + 1, :].astype(_F32) * hbp_ref[last:last + 1, :].astype(_F32)
    u_next = gcn_ref[0:1, :].astype(_F32) * hbn_ref[0:1, :].astype(_F32)
    u_prev = jnp.where(pos == 0, 0.0, u_prev)
    u_next = jnp.where(pos == tiles_per_seq - 1, 0.0, u_next)
    row = lax.broadcasted_iota(jnp.int32, (ts, 1), 0)
    up = jnp.where(row == 0, u_prev, pltpu.roll(u, 1, 0))
    un = jnp.where(row == ts - 1, u_next, pltpu.roll(u, ts - 1, 0))
    y = gb_ref[...].astype(_F32) * (w_ref[0:1, :] * up + w_ref[1:2, :] * u + w_ref[2:3, :] * un)
    o_ref[...] = (y * _row_rms_inv(y) * g_ref[...]).astype(o_ref.dtype)


def _mixer_b(proj_r, conv_w, gain, seq):
    m = proj_r.shape[0]
    wb = gain.shape[0]
    ts = _pick_tile(seq, (512, 256, 128))
    halo = BF16_SUBLANE_TILE
    per = ts // halo
    nhalo = m // halo
    main = lambda c: pl.BlockSpec((ts, wb), lambda i: (i, c))
    prev = lambda c: pl.BlockSpec((halo, wb), lambda i: (jnp.maximum(i * per - 1, 0), c))
    nxt = lambda c: pl.BlockSpec((halo, wb), lambda i: (jnp.minimum((i + 1) * per, nhalo - 1), c))
    return pl.pallas_call(
        functools.partial(_mixer_b_body, tiles_per_seq=seq // ts),
        grid=(m // ts,),
        in_specs=[main(0), main(1), main(2), prev(1), prev(2), nxt(1), nxt(2),
                  pl.BlockSpec((CONV_WIDTH, wb), lambda i: (0, 0)),
                  pl.BlockSpec((1, wb), lambda i: (0, 0))],
        out_specs=pl.BlockSpec((ts, wb), lambda i: (i, 0)),
        out_shape=jax.ShapeDtypeStruct((m, wb), _BF16),
        compiler_params=_params(("parallel",)),
        name="mixer_b",
    )(proj_r, proj_r, proj_r, proj_r, proj_r, proj_r, proj_r, conv_w, gain.reshape(1, wb))


def _rope_tables(seq):
    quarter = HEAD_DIM // 4
    inv = ROPE_THETA ** (-jnp.arange(quarter, dtype=_F32) / quarter)
    t = jnp.arange(seq)
    ang_r = (t // GRID_W).astype(_F32)[:, None] * inv[None, :]
    ang_c = (t % GRID_W).astype(_F32)[:, None] * inv[None, :]
    zero = jnp.zeros_like(ang_r)
    cos = jnp.concatenate([jnp.cos(ang_r), jnp.cos(ang_r), jnp.cos(ang_c), jnp.cos(ang_c)], axis=-1)
    sin_hi = jnp.concatenate([-jnp.sin(ang_r), zero, -jnp.sin(ang_c), zero], axis=-1)
    sin_lo = jnp.concatenate([zero, jnp.sin(ang_r), zero, jnp.sin(ang_c)], axis=-1)
    return cos, sin_hi, sin_lo


def _rope_body(x_ref, g_ref, cos_ref, shi_ref, slo_ref, o_ref, *, q_heads):
    hh = pl.program_id(1)
    quarter = HEAD_DIM // 4
    xv = x_ref[...].astype(_F32)
    gain = jnp.where(hh < q_heads, g_ref[0:1, :], g_ref[1:2, :])
    xn = xv * _row_rms_inv(xv) * gain
    rot = (xn * cos_ref[...]
           + pltpu.roll(xn, HEAD_DIM - quarter, 1) * shi_ref[...]
           + pltpu.roll(xn, quarter, 1) * slo_ref[...])
    scale = jnp.where(hh < q_heads, Q_SCALE * LOG2_E, 1.0)
    o_ref[...] = (rot * scale).astype(o_ref.dtype)


def _rope_qk(proj_r, gains, tables, seq, col0, q_heads, kv_heads):
    m = proj_r.shape[0]
    ts = _pick_tile(seq, (2048, 1024, 512, 256, 128))
    per_seq = seq // ts
    blk0 = col0 // HEAD_DIM
    nh = q_heads + kv_heads
    tab = pl.BlockSpec((ts, HEAD_DIM), lambda i, hh: (i % per_seq, 0))
    return pl.pallas_call(
        functools.partial(_rope_body, q_heads=q_heads),
        grid=(m // ts, nh),
        in_specs=[pl.BlockSpec((ts, HEAD_DIM), lambda i, hh: (i, blk0 + hh)),
                  pl.BlockSpec((2, HEAD_DIM), lambda i, hh: (0, 0)),
                  tab, tab, tab],
        out_specs=pl.BlockSpec((ts, HEAD_DIM), lambda i, hh: (i, hh)),
        out_shape=jax.ShapeDtypeStruct((m, nh * HEAD_DIM), _BF16),
        compiler_params=_params(("parallel", "parallel")),
        name="rope_qk",
    )(proj_r, gains, *tables)


def _attn_c_body(q_ref, k_ref, v_ref, o_ref, vt_ref, qt_ref, s0_ref, s1_ref, p_ref, acc_ref, *, tk):
    nchunk = vt_ref.shape[0]
    tq = q_ref.shape[0]
    nq = GQA_GROUP * tq

    @pl.when(pl.program_id(2) == 0)
    def _():
        def transpose_v(c, carry):
            rows = pl.ds(pl.multiple_of(c * tk, tk), tk)
            vt_ref[c] = v_ref[rows, :].astype(_F32).T.astype(_BF16)
            return carry

        lax.fori_loop(0, nchunk, transpose_v, 0)

    for g in range(GQA_GROUP):
        qt_ref[:, g * tq:(g + 1) * tq] = q_ref[:, g * HEAD_DIM:(g + 1) * HEAD_DIM].astype(_F32).T.astype(_BF16)
    acc_ref[...] = jnp.zeros(acc_ref.shape, _F32)

    def scores(c, s_ref):
        rows = pl.ds(pl.multiple_of(c * tk, tk), tk)
        s_ref[...] = jnp.dot(k_ref[rows, :], qt_ref[...], preferred_element_type=_F32)

    def consume(c, s_ref, m_old, l_old):
        m_new = jnp.maximum(m_old, jnp.max(s_ref[...], axis=0, keepdims=True))
        alpha = jnp.exp2(m_old - m_new)
        p = jnp.exp2(s_ref[...] - m_new)
        l_new = alpha * l_old + jnp.sum(p, axis=0, keepdims=True)
        p_ref[...] = p.astype(_BF16)
        acc_ref[...] = alpha * acc_ref[...] + jnp.dot(vt_ref[c], p_ref[...], preferred_element_type=_F32)
        return m_new, l_new

    scores(0, s0_ref)

    def pair(cc, carry):
        m_run, l_run = carry
        c0 = 2 * cc
        scores(c0 + 1, s1_ref)
        m_run, l_run = consume(c0, s0_ref, m_run, l_run)
        scores(jnp.minimum(c0 + 2, nchunk - 1), s0_ref)
        m_run, l_run = consume(c0 + 1, s1_ref, m_run, l_run)
        return m_run, l_run

    init = (jnp.full((1, nq), NEG, _F32), jnp.zeros((1, nq), _F32))
    _, l_fin = lax.fori_loop(0, nchunk // 2, pair, init)
    out_t = acc_ref[...] / l_fin
    for g in range(GQA_GROUP):
        o_ref[:, g * HEAD_DIM:(g + 1) * HEAD_DIM] = out_t[:, g * tq:(g + 1) * tq].T.astype(o_ref.dtype)


def _attn_c(qk, proj_r, batch, seq, q_heads, kv_heads, v_col0):
    m = qk.shape[0]
    assert q_heads == GQA_GROUP * kv_heads
    tq = _pick_tile(seq, (256, 128))
    tk = _pick_tile(seq, (512, 256, 128))
    nq = seq // tq
    gw = GQA_GROUP * HEAD_DIM
    v_blk0 = v_col0 // HEAD_DIM
    return pl.pallas_call(
        functools.partial(_attn_c_body, tk=tk),
        grid=(batch, kv_heads, nq),
        in_specs=[pl.BlockSpec((tq, gw), lambda b, g, i: (b * nq + i, g)),
                  pl.BlockSpec((seq, HEAD_DIM), lambda b, g, i: (b, q_heads + g)),
                  pl.BlockSpec((seq, HEAD_DIM), lambda b, g, i: (b, v_blk0 + g))],
        out_specs=pl.BlockSpec((tq, gw), lambda b, g, i: (b * nq + i, g)),
        out_shape=jax.ShapeDtypeStruct((m, q_heads * HEAD_DIM), _BF16),
        scratch_shapes=[pltpu.VMEM((seq // tk, HEAD_DIM, tk), _BF16),
                        pltpu.VMEM((HEAD_DIM, GQA_GROUP * tq), _BF16),
                        pltpu.VMEM((tk, GQA_GROUP * tq), _F32),
                        pltpu.VMEM((tk, GQA_GROUP * tq), _F32),
                        pltpu.VMEM((tk, GQA_GROUP * tq), _BF16),
                        pltpu.VMEM((HEAD_DIM, GQA_GROUP * tq), _F32)],
        compiler_params=_params(("parallel", "parallel", "arbitrary")),
        name="attn_c",
    )(qk, qk, proj_r)


def _out_proj_body(oa_ref, ob_ref, oc_ref, ga_ref, gc_ref, w_ref, x_ref, gp_ref, o_ref, xn_ref, *, tk):
    k = pl.program_id(1)
    nk = pl.num_programs(1)

    @pl.when(k == 0)
    def _():
        def fn(rows):
            c0 = 0
            for ref, g_ref in ((oa_ref, ga_ref), (ob_ref, None), (oc_ref, gc_ref)):
                width = ref.shape[1]
                if g_ref is None:
                    val = ref[rows, :]
                else:
                    vf = ref[rows, :].astype(_F32)
                    val = (vf * _row_rms_inv(vf) * g_ref[...]).astype(_BF16)
                for c in range(width // tk):
                    xn_ref[c0 + c, rows, :] = val[:, c * tk:(c + 1) * tk]
                c0 += width // tk

        _for_row_chunks(oa_ref.shape[0], fn)
        o_ref[...] = jnp.dot(xn_ref[0], w_ref[...], preferred_element_type=_F32)

    @pl.when(k > 0)
    def _():
        o_ref[...] += jnp.dot(xn_ref[k], w_ref[...], preferred_element_type=_F32)

    @pl.when(k == nk - 1)
    def _():
        _residual_norm_rows(o_ref, x_ref, gp_ref)


def _out_proj(oa, ob, oc, ga, gc, w, x, gp):
    m, d = x.shape
    kdim = w.shape[0]
    wa, wb, wc = oa.shape[1], ob.shape[1], oc.shape[1]
    tm = _pick_tile(m, (512, 256, 128))
    tk = next(t for t in (512, 256, 128) if wa % t == 0 and wb % t == 0 and wc % t == 0)
    nk = kdim // tk
    row = lambda width: pl.BlockSpec((tm, width), lambda i, k: (i, 0))
    vec = lambda width: pl.BlockSpec((1, width), lambda i, k: (0, 0))
    return pl.pallas_call(
        functools.partial(_out_proj_body, tk=tk),
        grid=(m // tm, nk),
        in_specs=[row(wa), row(wb), row(wc), vec(wa), vec(wc),
                  pl.BlockSpec((tk, d), lambda i, k: (k, 0)),
                  row(d), vec(d)],
        out_specs=row(d),
        out_shape=jax.ShapeDtypeStruct((m, d), _F32),
        scratch_shapes=[pltpu.VMEM((nk, tm, tk), _BF16)],
        compiler_params=_params(("parallel", "arbitrary")),
        name="out_proj",
    )(oa, ob, oc, ga.reshape(1, wa), gc.reshape(1, wc), w, x, gp.reshape(1, d))


def _mlp_body(x_ref, g1_ref, wu_ref, wd_ref, g2_ref, o_ref, xn_ref, hid_ref):
    f = pl.program_id(1)
    nf = pl.num_programs(1)

    @pl.when(f == 0)
    def _():
        _norm_rows_to(xn_ref, x_ref, g1_ref)

    hid = jnp.dot(xn_ref[...], wu_ref[...], preferred_element_type=_F32)
    hid_ref[...] = jnp.square(jnp.maximum(hid, 0.0)).astype(_BF16)

    @pl.when(f == 0)
    def _():
        o_ref[...] = jnp.dot(hid_ref[...], wd_ref[...], preferred_element_type=_F32)

    @pl.when(f > 0)
    def _():
        o_ref[...] += jnp.dot(hid_ref[...], wd_ref[...], preferred_element_type=_F32)

    @pl.when(f == nf - 1)
    def _():
        _residual_norm_rows(o_ref, x_ref, g2_ref)


def _mlp(x, g1, wu, wd, g2):
    m, d = x.shape
    dff = wu.shape[1]
    tm = _pick_tile(m, (512, 256, 128))
    tf = _pick_tile(dff, (512, 256, 128))
    row = pl.BlockSpec((tm, d), lambda i, f: (i, 0))
    vec = pl.BlockSpec((1, d), lambda i, f: (0, 0))
    return pl.pallas_call(
        _mlp_body,
        grid=(m // tm, dff // tf),
        in_specs=[row, vec,
                  pl.BlockSpec((d, tf), lambda i, f: (0, f)),
                  pl.BlockSpec((tf, d), lambda i, f: (f, 0)),
                  vec],
        out_specs=row,
        out_shape=jax.ShapeDtypeStruct((m, d), _F32),
        scratch_shapes=[pltpu.VMEM((tm, d), _BF16), pltpu.VMEM((tm, tf), _BF16)],
        compiler_params=_params(("parallel", "arbitrary")),
        name="mlp",
    )(x, g1.reshape(1, d), wu, wd, g2.reshape(1, d))


def kernel(x, rel_bias, pre_mix_norm, w_in, conv_w, q_norm, k_norm, out_norm_a, out_norm_b, out_norm_c,
           w_out, post_mix_norm, pre_mlp_norm, w_up, w_down, post_mlp_norm):
    batch, seq, d = x.shape
    depth = w_in.shape[0]
    wa, wb, wc = out_norm_a.shape[1], out_norm_b.shape[1], out_norm_c.shape[1]
    in_width = w_in.shape[2]
    kv_width = (in_width - 3 * wa - 3 * wb - wc) // 2
    a_heads, q_heads, kv_heads = wa // HEAD_DIM, wc // HEAD_DIM, kv_width // HEAD_DIM
    assert all(w // (2 * dil) == SPAN for w, dil in A_BRANCHES)
    assert rel_bias.shape == (NUM_BUCKETS, a_heads) and seq % GRID_W == 0

    idx_tbl = jnp.asarray(_bucket_table())
    tables = _rope_tables(seq)
    xf = x.reshape(batch * seq, d)
    for i in range(depth):
        proj_a, proj_r = _in_proj(xf, pre_mix_norm[i], w_in[i].astype(_BF16), 3 * wa)
        oa = _mixer_a(proj_a, rel_bias, idx_tbl, batch, seq, a_heads)
        ob = _mixer_b(proj_r, conv_w[i], out_norm_b[i], seq)
        qk = _rope_qk(proj_r, jnp.stack([q_norm[i], k_norm[i]]), tables, seq, 3 * wb, q_heads, kv_heads)
        oc = _attn_c(qk, proj_r, batch, seq, q_heads, kv_heads, 3 * wb + wc + kv_width)
        xf = _out_proj(oa, ob, oc, out_norm_a[i], out_norm_c[i], w_out[i].astype(_BF16), xf, post_mix_norm[i])
        xf = _mlp(xf, pre_mlp_norm[i], w_up[i].astype(_BF16), w_down[i].astype(_BF16), post_mlp_norm[i])
    return xf.reshape(batch, seq, d)
```

```python
import functools
import math

import numpy as np
import jax
import jax.numpy as jnp
from jax import lax
from jax.experimental import pallas as pl
from jax.experimental.pallas import tpu as pltpu

HEAD_DIM = 128
A_BRANCHES = ((128, 1), (512, 4), (2048, 16))
SPAN = 64
Q_TILE_A = 2 * SPAN
K_TILE_A = 4 * SPAN
GROUP_A = 4
ROPE_THETA = 10000.0
GRID_W = 64
NUM_BUCKETS = 32
MAX_DISTANCE = 1024
CONV_WIDTH = 3
GQA_GROUP = 4
EPS = 1e-6
NEG = -1e30
Q_SCALE = HEAD_DIM ** -0.5
LOG2_E = math.log2(math.e)

V7X_VMEM_LIMIT_BYTES = 60 * 1024 * 1024
BF16_SUBLANE_TILE = 16

_F32 = jnp.float32
_BF16 = jnp.bfloat16


def _pick_tile(n, prefs):
    for t in prefs:
        if n % t == 0:
            return t
    return n


def _params(sem):
    return pltpu.CompilerParams(dimension_semantics=sem, vmem_limit_bytes=V7X_VMEM_LIMIT_BYTES)


def _row_rms_inv(v):
    return lax.rsqrt(jnp.mean(v * v, axis=-1, keepdims=True) + EPS)


ROW_CHUNK = 16


def _for_row_chunks(nrows, fn):
    def step(c, carry):
        fn(pl.ds(pl.multiple_of(c * ROW_CHUNK, ROW_CHUNK), ROW_CHUNK))
        return carry

    lax.fori_loop(0, nrows // ROW_CHUNK, step, 0)


def _norm_rows_to(dst_ref, src_ref, g_ref):
    def fn(rows):
        v = src_ref[rows, :].astype(_F32)
        dst_ref[rows, :] = (v * _row_rms_inv(v) * g_ref[...]).astype(dst_ref.dtype)

    _for_row_chunks(src_ref.shape[0], fn)


def _residual_norm_rows(o_ref, x_ref, g_ref):
    def fn(rows):
        y = o_ref[rows, :]
        o_ref[rows, :] = x_ref[rows, :] + y * _row_rms_inv(y) * g_ref[...]

    _for_row_chunks(o_ref.shape[0], fn)


def _in_proj_body(x_ref, g_ref, w_ref, oa_ref, or_ref, xn_ref, *, n_a):
    j = pl.program_id(1)

    @pl.when(j == 0)
    def _():
        _norm_rows_to(xn_ref, x_ref, g_ref)

    @pl.when(j < n_a)
    def _():
        oa_ref[...] = jnp.dot(xn_ref[...], w_ref[...], preferred_element_type=_F32)

    @pl.when(j >= n_a)
    def _():
        or_ref[...] = jnp.dot(xn_ref[...], w_ref[...], preferred_element_type=_F32).astype(_BF16)


def _in_proj(x, g, w, a_cols):
    m, d = x.shape
    n = w.shape[1]
    tm = _pick_tile(m, (512, 256, 128))
    tn = next(t for t in (1024, 512, 256, 128) if a_cols % t == 0 and (n - a_cols) % t == 0)
    n_a = a_cols // tn
    return pl.pallas_call(
        functools.partial(_in_proj_body, n_a=n_a),
        grid=(m // tm, n // tn),
        in_specs=[
            pl.BlockSpec((tm, d), lambda i, j: (i, 0)),
            pl.BlockSpec((1, d), lambda i, j: (0, 0)),
            pl.BlockSpec((d, tn), lambda i, j: (0, j)),
        ],
        out_specs=[
            pl.BlockSpec((tm, tn), lambda i, j: (i, jnp.minimum(j, n_a - 1))),
            pl.BlockSpec((tm, tn), lambda i, j: (i, jnp.maximum(j - n_a, 0))),
        ],
        out_shape=[
            jax.ShapeDtypeStruct((m, a_cols), _F32),
            jax.ShapeDtypeStruct((m, n - a_cols), _BF16),
        ],
        scratch_shapes=[pltpu.VMEM((tm, d), _BF16)],
        compiler_params=_params(("parallel", "arbitrary")),
        name="in_proj",
    )(x, g.reshape(1, d), w)


def _bucket_table():
    half = NUM_BUCKETS // 2
    max_exact = half // 2
    i = np.arange(Q_TILE_A)[:, None]
    c = np.arange(K_TILE_A)[None, :]
    out = np.zeros((len(A_BRANCHES), 3, Q_TILE_A, K_TILE_A), np.int32)
    for br, (_, dil) in enumerate(A_BRANCHES):
        for var in range(3):
            rel = c - SPAN * var - i
            dist = rel * dil
            n = np.abs(dist)
            t = np.log(np.maximum(n, 1) / max_exact) / math.log(MAX_DISTANCE / max_exact) * (half - max_exact)
            valid = np.abs(rel) <= SPAN
            frac = np.abs(t - np.round(t))
            assert np.all((frac > 1e-4) | (n <= max_exact) | (n >= MAX_DISTANCE) | ~valid), "bucket edge near an integer"
            large = np.minimum(max_exact + t.astype(np.int32), half - 1)
            bucket = np.where(dist > 0, half, 0) + np.where(n < max_exact, n, large)
            out[br, var] = np.where(valid, bucket, NUM_BUCKETS)
    return out


def _mixer_a_body(rel_ref, idx_ref, q_ref, k_ref, v_ref, o_ref, bias_ref, acc_ref, m_ref, l_ref, *, seq):
    h = pl.program_id(1)

    for br in range(len(A_BRANCHES)):
        for var in range(3):
            idx = idx_ref[br, var]

            def fill(bkt, bias, idx=idx):
                return jnp.where(idx == bkt, rel_ref[bkt, h], bias)

            bias_ref[br, var] = lax.fori_loop(0, NUM_BUCKETS, fill, jnp.full(idx.shape, NEG, _F32))

    for br, (_, dil) in enumerate(A_BRANCHES):
        length = seq // dil
        nblk = length // Q_TILE_A
        first = br == 0

        def rows(start, size, dil=dil):
            if dil == 1:
                return pl.ds(pl.multiple_of(start, SPAN), size)
            return pl.ds(start, size, stride=dil)

        def group(blocks, br=br, dil=dil, length=length, nblk=nblk, first=first, rows=rows):
            qsls, ss, vs = [], [], []
            for n, r in blocks:
                p0 = n * Q_TILE_A
                ks = jnp.clip(p0 - SPAN, 0, length - K_TILE_A)
                var = jnp.where(n == 0, 0, jnp.where(n == nblk - 1, 2, 1))
                qsl = rows(r + dil * p0, Q_TILE_A)
                ksl = rows(r + dil * ks, K_TILE_A)
                q = (q_ref[qsl, :] * Q_SCALE).astype(_BF16)
                k = k_ref[ksl, :].astype(_BF16)
                s = lax.dot_general(q, k, (((1,), (1,)), ((), ())), preferred_element_type=_F32)
                qsls.append(qsl)
                ss.append(s + bias_ref[br, var])
                vs.append(v_ref[ksl, :].astype(_BF16))
            s = jnp.stack(ss)
            m_blk = jnp.max(s, axis=-1, keepdims=True)
            if first:
                m_new = m_blk
                p = jnp.exp(s - m_new)
            else:
                m_old = jnp.stack([m_ref[qsl, :] for qsl in qsls])
                m_new = jnp.maximum(m_old, m_blk)
                p = jnp.exp(s - jnp.concatenate([m_new] * (K_TILE_A // HEAD_DIM), axis=-1))
            l_blk = jnp.sum(p, axis=-1, keepdims=True)
            p = p.astype(_BF16)
            pv = jnp.stack([jnp.dot(p[j], vs[j], preferred_element_type=_F32) for j in range(GROUP_A)])
            if first:
                acc_new = pv
                l_new = jnp.broadcast_to(l_blk, pv.shape)
                m_new = jnp.broadcast_to(m_new, pv.shape)
            else:
                alpha = jnp.exp(m_old - m_new)
                acc_new = alpha * jnp.stack([acc_ref[qsl, :] for qsl in qsls]) + pv
                l_new = alpha * jnp.stack([l_ref[qsl, :] for qsl in qsls]) + l_blk
            for j, qsl in enumerate(qsls):
                acc_ref[qsl, :] = acc_new[j]
                l_ref[qsl, :] = l_new[j]
                m_ref[qsl, :] = m_new[j]

        if dil == 1:
            def body(n4, carry, group=group):
                group([(n4 * GROUP_A + j, 0) for j in range(GROUP_A)])
                return carry

            lax.fori_loop(0, nblk // GROUP_A, body, 0)
        else:
            def body(it, carry, group=group, nblk=nblk):
                rg, n = it // nblk, it % nblk
                group([(n, rg * GROUP_A + j) for j in range(GROUP_A)])
                return carry

            lax.fori_loop(0, (dil // GROUP_A) * nblk, body, 0)

    chunk = _pick_tile(seq, (512, 256, 128))

    def finish(c, carry):
        sl = pl.ds(pl.multiple_of(c * chunk, chunk), chunk)
        o_ref[sl, :] = (acc_ref[sl, :] / l_ref[sl, :]).astype(o_ref.dtype)
        return carry

    lax.fori_loop(0, seq // chunk, finish, 0)


def _mixer_a(proj_a, rel_bias, idx_tbl, batch, seq, heads):
    m = proj_a.shape[0]
    assert seq % (A_BRANCHES[-1][1] * K_TILE_A) == 0, "sequence too short for the widest dilation"
    assert all(dil == 1 or dil % GROUP_A == 0 for _, dil in A_BRANCHES) and (seq // Q_TILE_A) % GROUP_A == 0
    blk = (seq, HEAD_DIM)
    return pl.pallas_call(
        functools.partial(_mixer_a_body, seq=seq),
        grid=(batch, heads),
        in_specs=[
            pl.BlockSpec(memory_space=pltpu.SMEM),
            pl.BlockSpec(idx_tbl.shape, lambda b, h: (0, 0, 0, 0)),
            pl.BlockSpec(blk, lambda b, h: (b, h)),
            pl.BlockSpec(blk, lambda b, h: (b, heads + h)),
            pl.BlockSpec(blk, lambda b, h: (b, 2 * heads + h)),
        ],
        out_specs=pl.BlockSpec(blk, lambda b, h: (b, h)),
        out_shape=jax.ShapeDtypeStruct((m, heads * HEAD_DIM), _BF16),
        scratch_shapes=[
            pltpu.VMEM(idx_tbl.shape, _F32),
            pltpu.VMEM(blk, _F32),
            pltpu.VMEM(blk, _F32),
            pltpu.VMEM(blk, _F32),
        ],
        compiler_params=_params(("parallel", "parallel")),
        name="mixer_a",
    )(rel_bias, idx_tbl, proj_a, proj_a, proj_a)


def _mixer_b_body(gb_ref, gc_ref, hb_ref, gcp_ref, hbp_ref, gcn_ref, hbn_ref, w_ref, g_ref, o_ref, *, tiles_per_seq):
    i = pl.program_id(0)
    ts = gb_ref.shape[0]
    pos = i % tiles_per_seq
    u = gc_ref[...].astype(_F32) * hb_ref[...].astype(_F32)
    last = BF16_SUBLANE_TILE - 1
    u_prev = gcp_ref[last:last + 1, :].astype(_F32) * hbp_ref[last:last + 1, :].astype(_F32)
    u_next = gcn_ref[0:1, :].astype(_F32) * hbn_ref[0:1, :].astype(_F32)
    u_prev = jnp.where(pos == 0, 0.0, u_prev)
    u_next = jnp.where(pos == tiles_per_seq - 1, 0.0, u_next)
    row = lax.broadcasted_iota(jnp.int32, (ts, 1), 0)
    up = jnp.where(row == 0, u_prev, pltpu.roll(u, 1, 0))
    un = jnp.where(row == ts - 1, u_next, pltpu.roll(u, ts - 1, 0))
    y = gb_ref[...].astype(_F32) * (w_ref[0:1, :] * up + w_ref[1:2, :] * u + w_ref[2:3, :] * un)
    o_ref[...] = (y * _row_rms_inv(y) * g_ref[...]).astype(o_ref.dtype)


def _mixer_b(proj_r, conv_w, gain, seq):
    m = proj_r.shape[0]
    wb = gain.shape[0]
    ts = _pick_tile(seq, (512, 256, 128))
    halo = BF16_SUBLANE_TILE
    per = ts // halo
    nhalo = m // halo
    main = lambda c: pl.BlockSpec((ts, wb), lambda i: (i, c))
    prev = lambda c: pl.BlockSpec((halo, wb), lambda i: (jnp.maximum(i * per - 1, 0), c))
    nxt = lambda c: pl.BlockSpec((halo, wb), lambda i: (jnp.minimum((i + 1) * per, nhalo - 1), c))
    return pl.pallas_call(
        functools.partial(_mixer_b_body, tiles_per_seq=seq // ts),
        grid=(m // ts,),
        in_specs=[main(0), main(1), main(2), prev(1), prev(2), nxt(1), nxt(2),
                  pl.BlockSpec((CONV_WIDTH, wb), lambda i: (0, 0)),
                  pl.BlockSpec((1, wb), lambda i: (0, 0))],
        out_specs=pl.BlockSpec((ts, wb), lambda i: (i, 0)),
        out_shape=jax.ShapeDtypeStruct((m, wb), _BF16),
        compiler_params=_params(("parallel",)),
        name="mixer_b",
    )(proj_r, proj_r, proj_r, proj_r, proj_r, proj_r, proj_r, conv_w, gain.reshape(1, wb))


def _rope_tables(seq):
    quarter = HEAD_DIM // 4
    inv = ROPE_THETA ** (-jnp.arange(quarter, dtype=_F32) / quarter)
    t = jnp.arange(seq)
    ang_r = (t // GRID_W).astype(_F32)[:, None] * inv[None, :]
    ang_c = (t % GRID_W).astype(_F32)[:, None] * inv[None, :]
    zero = jnp.zeros_like(ang_r)
    cos = jnp.concatenate([jnp.cos(ang_r), jnp.cos(ang_r), jnp.cos(ang_c), jnp.cos(ang_c)], axis=-1)
    sin_hi = jnp.concatenate([-jnp.sin(ang_r), zero, -jnp.sin(ang_c), zero], axis=-1)
    sin_lo = jnp.concatenate([zero, jnp.sin(ang_r), zero, jnp.sin(ang_c)], axis=-1)
    return cos, sin_hi, sin_lo


def _rope_body(x_ref, g_ref, cos_ref, shi_ref, slo_ref, o_ref, *, q_heads):
    hh = pl.program_id(1)
    quarter = HEAD_DIM // 4
    xv = x_ref[...].astype(_F32)
    gain = jnp.where(hh < q_heads, g_ref[0:1, :], g_ref[1:2, :])
    xn = xv * _row_rms_inv(xv) * gain
    rot = (xn * cos_ref[...]
           + pltpu.roll(xn, HEAD_DIM - quarter, 1) * shi_ref[...]
           + pltpu.roll(xn, quarter, 1) * slo_ref[...])
    scale = jnp.where(hh < q_heads, Q_SCALE * LOG2_E, 1.0)
    o_ref[...] = (rot * scale).astype(o_ref.dtype)


def _rope_qk(proj_r, gains, tables, seq, col0, q_heads, kv_heads):
    m = proj_r.shape[0]
    ts = _pick_tile(seq, (2048, 1024, 512, 256, 128))
    per_seq = seq // ts
    blk0 = col0 // HEAD_DIM
    nh = q_heads + kv_heads
    tab = pl.BlockSpec((ts, HEAD_DIM), lambda i, hh: (i % per_seq, 0))
    return pl.pallas_call(
        functools.partial(_rope_body, q_heads=q_heads),
        grid=(m // ts, nh),
        in_specs=[pl.BlockSpec((ts, HEAD_DIM), lambda i, hh: (i, blk0 + hh)),
                  pl.BlockSpec((2, HEAD_DIM), lambda i, hh: (0, 0)),
                  tab, tab, tab],
        out_specs=pl.BlockSpec((ts, HEAD_DIM), lambda i, hh: (i, hh)),
        out_shape=jax.ShapeDtypeStruct((m, nh * HEAD_DIM), _BF16),
        compiler_params=_params(("parallel", "parallel")),
        name="rope_qk",
    )(proj_r, gains, *tables)


def _attn_c_body(q_ref, k_ref, v_ref, o_ref, vt_ref, qt_ref, s0_ref, s1_ref, p_ref, acc_ref, *, tk):
    nchunk = vt_ref.shape[0]
    tq = q_ref.shape[0]
    nq = GQA_GROUP * tq

    @pl.when(pl.program_id(2) == 0)
    def _():
        def transpose_v(c, carry):
            rows = pl.ds(pl.multiple_of(c * tk, tk), tk)
            vt_ref[c] = v_ref[rows, :].astype(_F32).T.astype(_BF16)
            return carry

        lax.fori_loop(0, nchunk, transpose_v, 0)

    for g in range(GQA_GROUP):
        qt_ref[:, g * tq:(g + 1) * tq] = q_ref[:, g * HEAD_DIM:(g + 1) * HEAD_DIM].astype(_F32).T.astype(_BF16)
    acc_ref[...] = jnp.zeros(acc_ref.shape, _F32)

    def scores(c, s_ref):
        rows = pl.ds(pl.multiple_of(c * tk, tk), tk)
        s_ref[...] = jnp.dot(k_ref[rows, :], qt_ref[...], preferred_element_type=_F32)

    def consume(c, s_ref, m_old, l_old):
        m_new = jnp.maximum(m_old, jnp.max(s_ref[...], axis=0, keepdims=True))
        alpha = jnp.exp2(m_old - m_new)
        p = jnp.exp2(s_ref[...] - m_new)
        l_new = alpha * l_old + jnp.sum(p, axis=0, keepdims=True)
        p_ref[...] = p.astype(_BF16)
        acc_ref[...] = alpha * acc_ref[...] + jnp.dot(vt_ref[c], p_ref[...], preferred_element_type=_F32)
        return m_new, l_new

    scores(0, s0_ref)

    def pair(cc, carry):
        m_run, l_run = carry
        c0 = 2 * cc
        scores(c0 + 1, s1_ref)
        m_run, l_run = consume(c0, s0_ref, m_run, l_run)
        scores(jnp.minimum(c0 + 2, nchunk - 1), s0_ref)
        m_run, l_run = consume(c0 + 1, s1_ref, m_run, l_run)
        return m_run, l_run

    init = (jnp.full((1, nq), NEG, _F32), jnp.zeros((1, nq), _F32))
    _, l_fin = lax.fori_loop(0, nchunk // 2, pair, init)
    out_t = acc_ref[...] / l_fin
    for g in range(GQA_GROUP):
        o_ref[:, g * HEAD_DIM:(g + 1) * HEAD_DIM] = out_t[:, g * tq:(g + 1) * tq].T.astype(o_ref.dtype)


def _attn_c(qk, proj_r, batch, seq, q_heads, kv_heads, v_col0):
    m = qk.shape[0]
    assert q_heads == GQA_GROUP * kv_heads
    tq = _pick_tile(seq, (256, 128))
    tk = _pick_tile(seq, (512, 256, 128))
    nq = seq // tq
    gw = GQA_GROUP * HEAD_DIM
    v_blk0 = v_col0 // HEAD_DIM
    return pl.pallas_call(
        functools.partial(_attn_c_body, tk=tk),
        grid=(batch, kv_heads, nq),
        in_specs=[pl.BlockSpec((tq, gw), lambda b, g, i: (b * nq + i, g)),
                  pl.BlockSpec((seq, HEAD_DIM), lambda b, g, i: (b, q_heads + g)),
                  pl.BlockSpec((seq, HEAD_DIM), lambda b, g, i: (b, v_blk0 + g))],
        out_specs=pl.BlockSpec((tq, gw), lambda b, g, i: (b * nq + i, g)),
        out_shape=jax.ShapeDtypeStruct((m, q_heads * HEAD_DIM), _BF16),
        scratch_shapes=[pltpu.VMEM((seq // tk, HEAD_DIM, tk), _BF16),
                        pltpu.VMEM((HEAD_DIM, GQA_GROUP * tq), _BF16),
                        pltpu.VMEM((tk, GQA_GROUP * tq), _F32),
                        pltpu.VMEM((tk, GQA_GROUP * tq), _F32),
                        pltpu.VMEM((tk, GQA_GROUP * tq), _BF16),
                        pltpu.VMEM((HEAD_DIM, GQA_GROUP * tq), _F32)],
        compiler_params=_params(("parallel", "parallel", "arbitrary")),
        name="attn_c",
    )(qk, qk, proj_r)


def _out_proj_body(oa_ref, ob_ref, oc_ref, ga_ref, gc_ref, w_ref, x_ref, gp_ref, o_ref, xn_ref, *, tk):
    k = pl.program_id(1)
    nk = pl.num_programs(1)

    @pl.when(k == 0)
    def _():
        def fn(rows):
            c0 = 0
            for ref, g_ref in ((oa_ref, ga_ref), (ob_ref, None), (oc_ref, gc_ref)):
                width = ref.shape[1]
                if g_ref is None:
                    val = ref[rows, :]
                else:
                    vf = ref[rows, :].astype(_F32)
                    val = (vf * _row_rms_inv(vf) * g_ref[...]).astype(_BF16)
                for c in range(width // tk):
                    xn_ref[c0 + c, rows, :] = val[:, c * tk:(c + 1) * tk]
                c0 += width // tk

        _for_row_chunks(oa_ref.shape[0], fn)
        o_ref[...] = jnp.dot(xn_ref[0], w_ref[...], preferred_element_type=_F32)

    @pl.when(k > 0)
    def _():
        o_ref[...] += jnp.dot(xn_ref[k], w_ref[...], preferred_element_type=_F32)

    @pl.when(k == nk - 1)
    def _():
        _residual_norm_rows(o_ref, x_ref, gp_ref)


def _out_proj(oa, ob, oc, ga, gc, w, x, gp):
    m, d = x.shape
    kdim = w.shape[0]
    wa, wb, wc = oa.shape[1], ob.shape[1], oc.shape[1]
    tm = _pick_tile(m, (512, 256, 128))
    tk = next(t for t in (512, 256, 128) if wa % t == 0 and wb % t == 0 and wc % t == 0)
    nk = kdim // tk
    row = lambda width: pl.BlockSpec((tm, width), lambda i, k: (i, 0))
    vec = lambda width: pl.BlockSpec((1, width), lambda i, k: (0, 0))
    return pl.pallas_call(
        functools.partial(_out_proj_body, tk=tk),
        grid=(m // tm, nk),
        in_specs=[row(wa), row(wb), row(wc), vec(wa), vec(wc),
                  pl.BlockSpec((tk, d), lambda i, k: (k, 0)),
                  row(d), vec(d)],
        out_specs=row(d),
        out_shape=jax.ShapeDtypeStruct((m, d), _F32),
        scratch_shapes=[pltpu.VMEM((nk, tm, tk), _BF16)],
        compiler_params=_params(("parallel", "arbitrary")),
        name="out_proj",
    )(oa, ob, oc, ga.reshape(1, wa), gc.reshape(1, wc), w, x, gp.reshape(1, d))


def _mlp_body(x_ref, g1_ref, wu_ref, wd_ref, g2_ref, o_ref, xn_ref, hid_ref):
    f = pl.program_id(1)
    nf = pl.num_programs(1)

    @pl.when(f == 0)
    def _():
        _norm_rows_to(xn_ref, x_ref, g1_ref)

    hid = jnp.dot(xn_ref[...], wu_ref[...], preferred_element_type=_F32)
    hid_ref[...] = jnp.square(jnp.maximum(hid, 0.0)).astype(_BF16)

    @pl.when(f == 0)
    def _():
        o_ref[...] = jnp.dot(hid_ref[...], wd_ref[...], preferred_element_type=_F32)

    @pl.when(f > 0)
    def _():
        o_ref[...] += jnp.dot(hid_ref[...], wd_ref[...], preferred_element_type=_F32)

    @pl.when(f == nf - 1)
    def _():
        _residual_norm_rows(o_ref, x_ref, g2_ref)


def _mlp(x, g1, wu, wd, g2):
    m, d = x.shape
    dff = wu.shape[1]
    tm = _pick_tile(m, (512, 256, 128))
    tf = _pick_tile(dff, (512, 256, 128))
    row = pl.BlockSpec((tm, d), lambda i, f: (i, 0))
    vec = pl.BlockSpec((1, d), lambda i, f: (0, 0))
    return pl.pallas_call(
        _mlp_body,
        grid=(m // tm, dff // tf),
        in_specs=[row, vec,
                  pl.BlockSpec((d, tf), lambda i, f: (0, f)),
                  pl.BlockSpec((tf, d), lambda i, f: (f, 0)),
                  vec],
        out_specs=row,
        out_shape=jax.ShapeDtypeStruct((m, d), _F32),
        scratch_shapes=[pltpu.VMEM((tm, d), _BF16), pltpu.VMEM((tm, tf), _BF16)],
        compiler_params=_params(("parallel", "arbitrary")),
        name="mlp",
    )(x, g1.reshape(1, d), wu, wd, g2.reshape(1, d))


def kernel(x, rel_bias, pre_mix_norm, w_in, conv_w, q_norm, k_norm, out_norm_a, out_norm_b, out_norm_c,
           w_out, post_mix_norm, pre_mlp_norm, w_up, w_down, post_mlp_norm):
    batch, seq, d = x.shape
    depth = w_in.shape[0]
    wa, wb, wc = out_norm_a.shape[1], out_norm_b.shape[1], out_norm_c.shape[1]
    in_width = w_in.shape[2]
    kv_width = (in_width - 3 * wa - 3 * wb - wc) // 2
    a_heads, q_heads, kv_heads = wa // HEAD_DIM, wc // HEAD_DIM, kv_width // HEAD_DIM
    assert all(w // (2 * dil) == SPAN for w, dil in A_BRANCHES)
    assert rel_bias.shape == (NUM_BUCKETS, a_heads) and seq % GRID_W == 0

    idx_tbl = jnp.asarray(_bucket_table())
    tables = _rope_tables(seq)
    xf = x.reshape(batch * seq, d)
    for i in range(depth):
        proj_a, proj_r = _in_proj(xf, pre_mix_norm[i], w_in[i].astype(_BF16), 3 * wa)
        oa = _mixer_a(proj_a, rel_bias, idx_tbl, batch, seq, a_heads)
        ob = _mixer_b(proj_r, conv_w[i], out_norm_b[i], seq)
        qk = _rope_qk(proj_r, jnp.stack([q_norm[i], k_norm[i]]), tables, seq, 3 * wb, q_heads, kv_heads)
        oc = _attn_c(qk, proj_r, batch, seq, q_heads, kv_heads, 3 * wb + wc + kv_width)
        xf = _out_proj(oa, ob, oc, out_norm_a[i], out_norm_c[i], w_out[i].astype(_BF16), xf, post_mix_norm[i])
        xf = _mlp(xf, pre_mlp_norm[i], w_up[i].astype(_BF16), w_down[i].astype(_BF16), post_mlp_norm[i])
    return xf.reshape(batch, seq, d)
```

```python
import functools
import math

import numpy as np
import jax
import jax.numpy as jnp
from jax import lax
from jax.experimental import pallas as pl
from jax.experimental.pallas import tpu as pltpu

HEAD_DIM = 128
A_BRANCHES = ((128, 1), (512, 4), (2048, 16))
SPAN = 64
Q_TILE_A = 2 * SPAN
K_TILE_A = 4 * SPAN
GROUP_A = 4
ROPE_THETA = 10000.0
GRID_W = 64
NUM_BUCKETS = 32
MAX_DISTANCE = 1024
CONV_WIDTH = 3
GQA_GROUP = 4
EPS = 1e-6
NEG = -1e30
Q_SCALE = HEAD_DIM ** -0.5
LOG2_E = math.log2(math.e)

V7X_VMEM_LIMIT_BYTES = 60 * 1024 * 1024
BF16_SUBLANE_TILE = 16

_F32 = jnp.float32
_BF16 = jnp.bfloat16


def _pick_tile(n, prefs):
    for t in prefs:
        if n % t == 0:
            return t
    return n


def _params(sem):
    return pltpu.CompilerParams(dimension_semantics=sem, vmem_limit_bytes=V7X_VMEM_LIMIT_BYTES)


def _row_rms_inv(v):
    return lax.rsqrt(jnp.mean(v * v, axis=-1, keepdims=True) + EPS)


ROW_CHUNK = 64
LANES = 128


def _for_row_chunks(nrows, fn):
    def step(c, carry):
        fn(pl.ds(pl.multiple_of(c * ROW_CHUNK, ROW_CHUNK), ROW_CHUNK))
        return carry

    lax.fori_loop(0, nrows // ROW_CHUNK, step, 0)


def _lane_tiles(width):
    return [slice(c, c + LANES) for c in range(0, width, LANES)]


def _rows_rms_inv(ref, rows):
    width = ref.shape[1]
    acc = None
    for cols in _lane_tiles(width):
        blk = ref[rows, cols].astype(_F32)
        acc = blk * blk if acc is None else acc + blk * blk
    inv = lax.rsqrt(jnp.sum(acc, axis=-1, keepdims=True) / width + EPS)
    return jnp.broadcast_to(inv, acc.shape)


def _norm_rows_to(dst_ref, src_ref, g_ref):
    def fn(rows):
        inv = _rows_rms_inv(src_ref, rows)
        for cols in _lane_tiles(src_ref.shape[1]):
            dst_ref[rows, cols] = (src_ref[rows, cols].astype(_F32) * inv * g_ref[:, cols]).astype(dst_ref.dtype)

    _for_row_chunks(src_ref.shape[0], fn)


def _residual_norm_rows(o_ref, x_ref, g_ref):
    def fn(rows):
        inv = _rows_rms_inv(o_ref, rows)
        for cols in _lane_tiles(o_ref.shape[1]):
            o_ref[rows, cols] = x_ref[rows, cols] + o_ref[rows, cols] * inv * g_ref[:, cols]

    _for_row_chunks(o_ref.shape[0], fn)


def _in_proj_body(x_ref, g_ref, w_ref, oa_ref, or_ref, xn_ref, *, n_a):
    j = pl.program_id(1)

    @pl.when(j == 0)
    def _():
        _norm_rows_to(xn_ref, x_ref, g_ref)

    @pl.when(j < n_a)
    def _():
        oa_ref[...] = jnp.dot(xn_ref[...], w_ref[...], preferred_element_type=_F32)

    @pl.when(j >= n_a)
    def _():
        or_ref[...] = jnp.dot(xn_ref[...], w_ref[...], preferred_element_type=_F32).astype(_BF16)


def _in_proj(x, g, w, a_cols):
    m, d = x.shape
    n = w.shape[1]
    tm = _pick_tile(m, (512, 256, 128))
    tn = next(t for t in (1024, 512, 256, 128) if a_cols % t == 0 and (n - a_cols) % t == 0)
    n_a = a_cols // tn
    return pl.pallas_call(
        functools.partial(_in_proj_body, n_a=n_a),
        grid=(m // tm, n // tn),
        in_specs=[
            pl.BlockSpec((tm, d), lambda i, j: (i, 0)),
            pl.BlockSpec((1, d), lambda i, j: (0, 0)),
            pl.BlockSpec((d, tn), lambda i, j: (0, j)),
        ],
        out_specs=[
            pl.BlockSpec((tm, tn), lambda i, j: (i, jnp.minimum(j, n_a - 1))),
            pl.BlockSpec((tm, tn), lambda i, j: (i, jnp.maximum(j - n_a, 0))),
        ],
        out_shape=[
            jax.ShapeDtypeStruct((m, a_cols), _F32),
            jax.ShapeDtypeStruct((m, n - a_cols), _BF16),
        ],
        scratch_shapes=[pltpu.VMEM((tm, d), _BF16)],
        compiler_params=_params(("parallel", "arbitrary")),
        name="in_proj",
    )(x, g.reshape(1, d), w)


def _bucket_table():
    half = NUM_BUCKETS // 2
    max_exact = half // 2
    i = np.arange(Q_TILE_A)[:, None]
    c = np.arange(K_TILE_A)[None, :]
    out = np.zeros((len(A_BRANCHES), 3, Q_TILE_A, K_TILE_A), np.int32)
    for br, (_, dil) in enumerate(A_BRANCHES):
        for var in range(3):
            rel = c - SPAN * var - i
            dist = rel * dil
            n = np.abs(dist)
            t = np.log(np.maximum(n, 1) / max_exact) / math.log(MAX_DISTANCE / max_exact) * (half - max_exact)
            valid = np.abs(rel) <= SPAN
            frac = np.abs(t - np.round(t))
            assert np.all((frac > 1e-4) | (n <= max_exact) | (n >= MAX_DISTANCE) | ~valid), "bucket edge near an integer"
            large = np.minimum(max_exact + t.astype(np.int32), half - 1)
            bucket = np.where(dist > 0, half, 0) + np.where(n < max_exact, n, large)
            out[br, var] = np.where(valid, bucket, NUM_BUCKETS)
    return out


def _mixer_a_body(rel_ref, idx_ref, q_ref, k_ref, v_ref, o_ref, bias_ref, acc_ref, m_ref, l_ref, *, seq):
    h = pl.program_id(1)

    for br in range(len(A_BRANCHES)):
        for var in range(3):
            idx = idx_ref[br, var]

            def fill(bkt, bias, idx=idx):
                return jnp.where(idx == bkt, rel_ref[bkt, h], bias)

            bias_ref[br, var] = lax.fori_loop(0, NUM_BUCKETS, fill, jnp.full(idx.shape, NEG, _F32))

    for br, (_, dil) in enumerate(A_BRANCHES):
        length = seq // dil
        nblk = length // Q_TILE_A
        first = br == 0

        def rows(start, size, dil=dil):
            if dil == 1:
                return pl.ds(pl.multiple_of(start, SPAN), size)
            return pl.ds(start, size, stride=dil)

        def group(blocks, br=br, dil=dil, length=length, nblk=nblk, first=first, rows=rows):
            qsls, ss, vs = [], [], []
            for n, r in blocks:
                p0 = n * Q_TILE_A
                ks = jnp.clip(p0 - SPAN, 0, length - K_TILE_A)
                var = jnp.where(n == 0, 0, jnp.where(n == nblk - 1, 2, 1))
                qsl = rows(r + dil * p0, Q_TILE_A)
                ksl = rows(r + dil * ks, K_TILE_A)
                q = (q_ref[qsl, :] * Q_SCALE).astype(_BF16)
                k = k_ref[ksl, :].astype(_BF16)
                s = lax.dot_general(q, k, (((1,), (1,)), ((), ())), preferred_element_type=_F32)
                qsls.append(qsl)
                ss.append(s + bias_ref[br, var])
                vs.append(v_ref[ksl, :].astype(_BF16))
            s = jnp.stack(ss)
            m_blk = jnp.max(s, axis=-1, keepdims=True)
            if first:
                m_new = m_blk
                p = jnp.exp(s - m_new)
            else:
                m_old = jnp.stack([m_ref[qsl, :] for qsl in qsls])
                m_new = jnp.maximum(m_old, m_blk)
                p = jnp.exp(s - jnp.concatenate([m_new] * (K_TILE_A // HEAD_DIM), axis=-1))
            l_blk = jnp.sum(p, axis=-1, keepdims=True)
            p = p.astype(_BF16)
            pv = jnp.stack([jnp.dot(p[j], vs[j], preferred_element_type=_F32) for j in range(GROUP_A)])
            if first:
                acc_new = pv
                l_new = jnp.broadcast_to(l_blk, pv.shape)
                m_new = jnp.broadcast_to(m_new, pv.shape)
            else:
                alpha = jnp.exp(m_old - m_new)
                acc_new = alpha * jnp.stack([acc_ref[qsl, :] for qsl in qsls]) + pv
                l_new = alpha * jnp.stack([l_ref[qsl, :] for qsl in qsls]) + l_blk
            for j, qsl in enumerate(qsls):
                acc_ref[qsl, :] = acc_new[j]
                l_ref[qsl, :] = l_new[j]
                m_ref[qsl, :] = m_new[j]

        if dil == 1:
            def body(n4, carry, group=group):
                group([(n4 * GROUP_A + j, 0) for j in range(GROUP_A)])
                return carry

            lax.fori_loop(0, nblk // GROUP_A, body, 0)
        else:
            def body(it, carry, group=group, nblk=nblk):
                rg, n = it // nblk, it % nblk
                group([(n, rg * GROUP_A + j) for j in range(GROUP_A)])
                return carry

            lax.fori_loop(0, (dil // GROUP_A) * nblk, body, 0)

    chunk = _pick_tile(seq, (512, 256, 128))

    def finish(c, carry):
        sl = pl.ds(pl.multiple_of(c * chunk, chunk), chunk)
        o_ref[sl, :] = (acc_ref[sl, :] / l_ref[sl, :]).astype(o_ref.dtype)
        return carry

    lax.fori_loop(0, seq // chunk, finish, 0)


def _mixer_a(proj_a, rel_bias, idx_tbl, batch, seq, heads):
    m = proj_a.shape[0]
    assert seq % (A_BRANCHES[-1][1] * K_TILE_A) == 0, "sequence too short for the widest dilation"
    assert all(dil == 1 or dil % GROUP_A == 0 for _, dil in A_BRANCHES) and (seq // Q_TILE_A) % GROUP_A == 0
    blk = (seq, HEAD_DIM)
    return pl.pallas_call(
        functools.partial(_mixer_a_body, seq=seq),
        grid=(batch, heads),
        in_specs=[
            pl.BlockSpec(memory_space=pltpu.SMEM),
            pl.BlockSpec(idx_tbl.shape, lambda b, h: (0, 0, 0, 0)),
            pl.BlockSpec(blk, lambda b, h: (b, h)),
            pl.BlockSpec(blk, lambda b, h: (b, heads + h)),
            pl.BlockSpec(blk, lambda b, h: (b, 2 * heads + h)),
        ],
        out_specs=pl.BlockSpec(blk, lambda b, h: (b, h)),
        out_shape=jax.ShapeDtypeStruct((m, heads * HEAD_DIM), _BF16),
        scratch_shapes=[
            pltpu.VMEM(idx_tbl.shape, _F32),
            pltpu.VMEM(blk, _F32),
            pltpu.VMEM(blk, _F32),
            pltpu.VMEM(blk, _F32),
        ],
        compiler_params=_params(("parallel", "parallel")),
        name="mixer_a",
    )(rel_bias, idx_tbl, proj_a, proj_a, proj_a)


def _mixer_b_body(gb_ref, gc_ref, hb_ref, gcp_ref, hbp_ref, gcn_ref, hbn_ref, w_ref, g_ref, o_ref, *, tiles_per_seq):
    i = pl.program_id(0)
    ts = gb_ref.shape[0]
    pos = i % tiles_per_seq
    u = gc_ref[...].astype(_F32) * hb_ref[...].astype(_F32)
    last = BF16_SUBLANE_TILE - 1
    u_prev = gcp_ref[last:last + 1, :].astype(_F32) * hbp_ref[last:last + 1, :].astype(_F32)
    u_next = gcn_ref[0:1, :].astype(_F32) * hbn_ref[0:1, :].astype(_F32)
    u_prev = jnp.where(pos == 0, 0.0, u_prev)
    u_next = jnp.where(pos == tiles_per_seq - 1, 0.0, u_next)
    row = lax.broadcasted_iota(jnp.int32, (ts, 1), 0)
    up = jnp.where(row == 0, u_prev, pltpu.roll(u, 1, 0))
    un = jnp.where(row == ts - 1, u_next, pltpu.roll(u, ts - 1, 0))
    y = gb_ref[...].astype(_F32) * (w_ref[0:1, :] * up + w_ref[1:2, :] * u + w_ref[2:3, :] * un)
    o_ref[...] = (y * _row_rms_inv(y) * g_ref[...]).astype(o_ref.dtype)


def _mixer_b(proj_r, conv_w, gain, seq):
    m = proj_r.shape[0]
    wb = gain.shape[0]
    ts = _pick_tile(seq, (512, 256, 128))
    halo = BF16_SUBLANE_TILE
    per = ts // halo
    nhalo = m // halo
    main = lambda c: pl.BlockSpec((ts, wb), lambda i: (i, c))
    prev = lambda c: pl.BlockSpec((halo, wb), lambda i: (jnp.maximum(i * per - 1, 0), c))
    nxt = lambda c: pl.BlockSpec((halo, wb), lambda i: (jnp.minimum((i + 1) * per, nhalo - 1), c))
    return pl.pallas_call(
        functools.partial(_mixer_b_body, tiles_per_seq=seq // ts),
        grid=(m // ts,),
        in_specs=[main(0), main(1), main(2), prev(1), prev(2), nxt(1), nxt(2),
                  pl.BlockSpec((CONV_WIDTH, wb), lambda i: (0, 0)),
                  pl.BlockSpec((1, wb), lambda i: (0, 0))],
        out_specs=pl.BlockSpec((ts, wb), lambda i: (i, 0)),
        out_shape=jax.ShapeDtypeStruct((m, wb), _BF16),
        compiler_params=_params(("parallel",)),
        name="mixer_b",
    )(proj_r, proj_r, proj_r, proj_r, proj_r, proj_r, proj_r, conv_w, gain.reshape(1, wb))


def _rope_tables(seq):
    quarter = HEAD_DIM // 4
    inv = ROPE_THETA ** (-jnp.arange(quarter, dtype=_F32) / quarter)
    t = jnp.arange(seq)
    ang_r = (t // GRID_W).astype(_F32)[:, None] * inv[None, :]
    ang_c = (t % GRID_W).astype(_F32)[:, None] * inv[None, :]
    zero = jnp.zeros_like(ang_r)
    cos = jnp.concatenate([jnp.cos(ang_r), jnp.cos(ang_r), jnp.cos(ang_c), jnp.cos(ang_c)], axis=-1)
    sin_hi = jnp.concatenate([-jnp.sin(ang_r), zero, -jnp.sin(ang_c), zero], axis=-1)
    sin_lo = jnp.concatenate([zero, jnp.sin(ang_r), zero, jnp.sin(ang_c)], axis=-1)
    return cos, sin_hi, sin_lo


def _rope_body(x_ref, g_ref, cos_ref, shi_ref, slo_ref, o_ref, *, q_heads):
    hh = pl.program_id(1)
    quarter = HEAD_DIM // 4
    xv = x_ref[...].astype(_F32)
    gain = jnp.where(hh < q_heads, g_ref[0:1, :], g_ref[1:2, :])
    xn = xv * _row_rms_inv(xv) * gain
    rot = (xn * cos_ref[...]
           + pltpu.roll(xn, HEAD_DIM - quarter, 1) * shi_ref[...]
           + pltpu.roll(xn, quarter, 1) * slo_ref[...])
    scale = jnp.where(hh < q_heads, Q_SCALE * LOG2_E, 1.0)
    o_ref[...] = (rot * scale).astype(o_ref.dtype)


def _rope_qk(proj_r, gains, tables, seq, col0, q_heads, kv_heads):
    m = proj_r.shape[0]
    ts = _pick_tile(seq, (2048, 1024, 512, 256, 128))
    per_seq = seq // ts
    blk0 = col0 // HEAD_DIM
    nh = q_heads + kv_heads
    tab = pl.BlockSpec((ts, HEAD_DIM), lambda i, hh: (i % per_seq, 0))
    return pl.pallas_call(
        functools.partial(_rope_body, q_heads=q_heads),
        grid=(m // ts, nh),
        in_specs=[pl.BlockSpec((ts, HEAD_DIM), lambda i, hh: (i, blk0 + hh)),
                  pl.BlockSpec((2, HEAD_DIM), lambda i, hh: (0, 0)),
                  tab, tab, tab],
        out_specs=pl.BlockSpec((ts, HEAD_DIM), lambda i, hh: (i, hh)),
        out_shape=jax.ShapeDtypeStruct((m, nh * HEAD_DIM), _BF16),
        compiler_params=_params(("parallel", "parallel")),
        name="rope_qk",
    )(proj_r, gains, *tables)


def _attn_c_body(q_ref, k_ref, v_ref, o_ref, vt_ref, qt_ref, s0_ref, s1_ref, p_ref, acc_ref, *, tk):
    nchunk = vt_ref.shape[0]
    tq = q_ref.shape[0]
    nq = GQA_GROUP * tq

    @pl.when(pl.program_id(2) == 0)
    def _():
        def transpose_v(c, carry):
            rows = pl.ds(pl.multiple_of(c * tk, tk), tk)
            vt_ref[c] = v_ref[rows, :].astype(_F32).T.astype(_BF16)
            return carry

        lax.fori_loop(0, nchunk, transpose_v, 0)

    for g in range(GQA_GROUP):
        qt_ref[:, g * tq:(g + 1) * tq] = q_ref[:, g * HEAD_DIM:(g + 1) * HEAD_DIM].astype(_F32).T.astype(_BF16)
    acc_ref[...] = jnp.zeros(acc_ref.shape, _F32)

    def scores(c, s_ref):
        rows = pl.ds(pl.multiple_of(c * tk, tk), tk)
        s_ref[...] = jnp.dot(k_ref[rows, :], qt_ref[...], preferred_element_type=_F32)

    def consume(c, s_ref, m_old, l_old):
        m_new = jnp.maximum(m_old, jnp.max(s_ref[...], axis=0, keepdims=True))
        alpha = jnp.exp2(m_old - m_new)
        p = jnp.exp2(s_ref[...] - m_new)
        l_new = alpha * l_old + jnp.sum(p, axis=0, keepdims=True)
        p_ref[...] = p.astype(_BF16)
        acc_ref[...] = alpha * acc_ref[...] + jnp.dot(vt_ref[c], p_ref[...], preferred_element_type=_F32)
        return m_new, l_new

    scores(0, s0_ref)

    def pair(cc, carry):
        m_run, l_run = carry
        c0 = 2 * cc
        scores(c0 + 1, s1_ref)
        m_run, l_run = consume(c0, s0_ref, m_run, l_run)
        scores(jnp.minimum(c0 + 2, nchunk - 1), s0_ref)
        m_run, l_run = consume(c0 + 1, s1_ref, m_run, l_run)
        return m_run, l_run

    init = (jnp.full((1, nq), NEG, _F32), jnp.zeros((1, nq), _F32))
    _, l_fin = lax.fori_loop(0, nchunk // 2, pair, init)
    out_t = acc_ref[...] / l_fin
    for g in range(GQA_GROUP):
        o_ref[:, g * HEAD_DIM:(g + 1) * HEAD_DIM] = out_t[:, g * tq:(g + 1) * tq].T.astype(o_ref.dtype)


def _attn_c(qk, proj_r, batch, seq, q_heads, kv_heads, v_col0):
    m = qk.shape[0]
    assert q_heads == GQA_GROUP * kv_heads
    tq = _pick_tile(seq, (256, 128))
    tk = _pick_tile(seq, (512, 256, 128))
    nq = seq // tq
    gw = GQA_GROUP * HEAD_DIM
    v_blk0 = v_col0 // HEAD_DIM
    return pl.pallas_call(
        functools.partial(_attn_c_body, tk=tk),
        grid=(batch, kv_heads, nq),
        in_specs=[pl.BlockSpec((tq, gw), lambda b, g, i: (b * nq + i, g)),
                  pl.BlockSpec((seq, HEAD_DIM), lambda b, g, i: (b, q_heads + g)),
                  pl.BlockSpec((seq, HEAD_DIM), lambda b, g, i: (b, v_blk0 + g))],
        out_specs=pl.BlockSpec((tq, gw), lambda b, g, i: (b * nq + i, g)),
        out_shape=jax.ShapeDtypeStruct((m, q_heads * HEAD_DIM), _BF16),
        scratch_shapes=[pltpu.VMEM((seq // tk, HEAD_DIM, tk), _BF16),
                        pltpu.VMEM((HEAD_DIM, GQA_GROUP * tq), _BF16),
                        pltpu.VMEM((tk, GQA_GROUP * tq), _F32),
                        pltpu.VMEM((tk, GQA_GROUP * tq), _F32),
                        pltpu.VMEM((tk, GQA_GROUP * tq), _BF16),
                        pltpu.VMEM((HEAD_DIM, GQA_GROUP * tq), _F32)],
        compiler_params=_params(("parallel", "parallel", "arbitrary")),
        name="attn_c",
    )(qk, qk, proj_r)


def _out_proj_body(oa_ref, ob_ref, oc_ref, ga_ref, gc_ref, w_ref, x_ref, gp_ref, o_ref, xn_ref, *, tk):
    k = pl.program_id(1)
    nk = pl.num_programs(1)

    @pl.when(k == 0)
    def _():
        def fn(rows):
            c0 = 0
            for ref, g_ref in ((oa_ref, ga_ref), (ob_ref, None), (oc_ref, gc_ref)):
                inv = None if g_ref is None else _rows_rms_inv(ref, rows)
                for cols in _lane_tiles(ref.shape[1]):
                    val = ref[rows, cols]
                    if g_ref is not None:
                        val = (val.astype(_F32) * inv * g_ref[:, cols]).astype(_BF16)
                    xn_ref[c0 + cols.start // tk, rows, cols.start % tk:cols.start % tk + LANES] = val
                c0 += ref.shape[1] // tk

        _for_row_chunks(oa_ref.shape[0], fn)
        o_ref[...] = jnp.dot(xn_ref[0], w_ref[...], preferred_element_type=_F32)

    @pl.when(k > 0)
    def _():
        o_ref[...] += jnp.dot(xn_ref[k], w_ref[...], preferred_element_type=_F32)

    @pl.when(k == nk - 1)
    def _():
        _residual_norm_rows(o_ref, x_ref, gp_ref)


def _out_proj(oa, ob, oc, ga, gc, w, x, gp):
    m, d = x.shape
    kdim = w.shape[0]
    wa, wb, wc = oa.shape[1], ob.shape[1], oc.shape[1]
    tm = _pick_tile(m, (512, 256, 128))
    tk = next(t for t in (512, 256, 128) if wa % t == 0 and wb % t == 0 and wc % t == 0)
    nk = kdim // tk
    row = lambda width: pl.BlockSpec((tm, width), lambda i, k: (i, 0))
    vec = lambda width: pl.BlockSpec((1, width), lambda i, k: (0, 0))
    return pl.pallas_call(
        functools.partial(_out_proj_body, tk=tk),
        grid=(m // tm, nk),
        in_specs=[row(wa), row(wb), row(wc), vec(wa), vec(wc),
                  pl.BlockSpec((tk, d), lambda i, k: (k, 0)),
                  row(d), vec(d)],
        out_specs=row(d),
        out_shape=jax.ShapeDtypeStruct((m, d), _F32),
        scratch_shapes=[pltpu.VMEM((nk, tm, tk), _BF16)],
        compiler_params=_params(("parallel", "arbitrary")),
        name="out_proj",
    )(oa, ob, oc, ga.reshape(1, wa), gc.reshape(1, wc), w, x, gp.reshape(1, d))


def _mlp_body(x_ref, g1_ref, wu_ref, wd_ref, g2_ref, o_ref, xn_ref, hid_ref):
    f = pl.program_id(1)
    nf = pl.num_programs(1)

    @pl.when(f == 0)
    def _():
        _norm_rows_to(xn_ref, x_ref, g1_ref)

    hid = jnp.dot(xn_ref[...], wu_ref[...], preferred_element_type=_F32)
    hid_ref[...] = jnp.square(jnp.maximum(hid, 0.0)).astype(_BF16)

    @pl.when(f == 0)
    def _():
        o_ref[...] = jnp.dot(hid_ref[...], wd_ref[...], preferred_element_type=_F32)

    @pl.when(f > 0)
    def _():
        o_ref[...] += jnp.dot(hid_ref[...], wd_ref[...], preferred_element_type=_F32)

    @pl.when(f == nf - 1)
    def _():
        _residual_norm_rows(o_ref, x_ref, g2_ref)


def _mlp(x, g1, wu, wd, g2):
    m, d = x.shape
    dff = wu.shape[1]
    tm = _pick_tile(m, (512, 256, 128))
    tf = _pick_tile(dff, (512, 256, 128))
    row = pl.BlockSpec((tm, d), lambda i, f: (i, 0))
    vec = pl.BlockSpec((1, d), lambda i, f: (0, 0))
    return pl.pallas_call(
        _mlp_body,
        grid=(m // tm, dff // tf),
        in_specs=[row, vec,
                  pl.BlockSpec((d, tf), lambda i, f: (0, f)),
                  pl.BlockSpec((tf, d), lambda i, f: (f, 0)),
                  vec],
        out_specs=row,
        out_shape=jax.ShapeDtypeStruct((m, d), _F32),
        scratch_shapes=[pltpu.VMEM((tm, d), _BF16), pltpu.VMEM((tm, tf), _BF16)],
        compiler_params=_params(("parallel", "arbitrary")),
        name="mlp",
    )(x, g1.reshape(1, d), wu, wd, g2.reshape(1, d))


def kernel(x, rel_bias, pre_mix_norm, w_in, conv_w, q_norm, k_norm, out_norm_a, out_norm_b, out_norm_c,
           w_out, post_mix_norm, pre_mlp_norm, w_up, w_down, post_mlp_norm):
    batch, seq, d = x.shape
    depth = w_in.shape[0]
    wa, wb, wc = out_norm_a.shape[1], out_norm_b.shape[1], out_norm_c.shape[1]
    in_width = w_in.shape[2]
    kv_width = (in_width - 3 * wa - 3 * wb - wc) // 2
    a_heads, q_heads, kv_heads = wa // HEAD_DIM, wc // HEAD_DIM, kv_width // HEAD_DIM
    assert all(w // (2 * dil) == SPAN for w, dil in A_BRANCHES)
    assert rel_bias.shape == (NUM_BUCKETS, a_heads) and seq % GRID_W == 0

    idx_tbl = jnp.asarray(_bucket_table())
    tables = _rope_tables(seq)
    xf = x.reshape(batch * seq, d)
    for i in range(depth):
        proj_a, proj_r = _in_proj(xf, pre_mix_norm[i], w_in[i].astype(_BF16), 3 * wa)
        oa = _mixer_a(proj_a, rel_bias, idx_tbl, batch, seq, a_heads)
        ob = _mixer_b(proj_r, conv_w[i], out_norm_b[i], seq)
        qk = _rope_qk(proj_r, jnp.stack([q_norm[i], k_norm[i]]), tables, seq, 3 * wb, q_heads, kv_heads)
        oc = _attn_c(qk, proj_r, batch, seq, q_heads, kv_heads, 3 * wb + wc + kv_width)
        xf = _out_proj(oa, ob, oc, out_norm_a[i], out_norm_c[i], w_out[i].astype(_BF16), xf, post_mix_norm[i])
        xf = _mlp(xf, pre_mlp_norm[i], w_up[i].astype(_BF16), w_down[i].astype(_BF16), post_mlp_norm[i])
    return xf.reshape(batch, seq, d)
```

```python
import functools
import math

import numpy as np
import jax
import jax.numpy as jnp
from jax import lax
from jax.experimental import pallas as pl
from jax.experimental.pallas import tpu as pltpu

HEAD_DIM = 128
A_BRANCHES = ((128, 1), (512, 4), (2048, 16))
SPAN = 64
Q_TILE_A = 2 * SPAN
K_TILE_A = 4 * SPAN
GROUP_A = 4
ROPE_THETA = 10000.0
GRID_W = 64
NUM_BUCKETS = 32
MAX_DISTANCE = 1024
CONV_WIDTH = 3
GQA_GROUP = 4
EPS = 1e-6
NEG = -1e30
Q_SCALE = HEAD_DIM ** -0.5
LOG2_E = math.log2(math.e)

V7X_VMEM_LIMIT_BYTES = 60 * 1024 * 1024
BF16_SUBLANE_TILE = 16

_F32 = jnp.float32
_BF16 = jnp.bfloat16


def _pick_tile(n, prefs):
    for t in prefs:
        if n % t == 0:
            return t
    return n


def _params(sem):
    return pltpu.CompilerParams(dimension_semantics=sem, vmem_limit_bytes=V7X_VMEM_LIMIT_BYTES)


def _row_rms_inv(v):
    return lax.rsqrt(jnp.mean(v * v, axis=-1, keepdims=True) + EPS)


ROW_CHUNK = 64
LANES = 128


def _for_row_chunks(nrows, fn):
    def step(c, carry):
        fn(pl.ds(pl.multiple_of(c * ROW_CHUNK, ROW_CHUNK), ROW_CHUNK))
        return carry

    lax.fori_loop(0, nrows // ROW_CHUNK, step, 0)


def _lane_tiles(width):
    return [slice(c, c + LANES) for c in range(0, width, LANES)]


def _rows_rms_inv(ref, rows):
    width = ref.shape[1]
    acc = None
    for cols in _lane_tiles(width):
        blk = ref[rows, cols].astype(_F32)
        acc = blk * blk if acc is None else acc + blk * blk
    inv = lax.rsqrt(jnp.sum(acc, axis=-1, keepdims=True) / width + EPS)
    return jnp.broadcast_to(inv, acc.shape)


def _norm_rows_to(dst_ref, src_ref, g_ref):
    def fn(rows):
        inv = _rows_rms_inv(src_ref, rows)
        for cols in _lane_tiles(src_ref.shape[1]):
            dst_ref[rows, cols] = (src_ref[rows, cols].astype(_F32) * inv * g_ref[:, cols]).astype(dst_ref.dtype)

    _for_row_chunks(src_ref.shape[0], fn)


def _residual_norm_rows(o_ref, x_ref, g_ref):
    def fn(rows):
        inv = _rows_rms_inv(o_ref, rows)
        for cols in _lane_tiles(o_ref.shape[1]):
            o_ref[rows, cols] = x_ref[rows, cols] + o_ref[rows, cols] * inv * g_ref[:, cols]

    _for_row_chunks(o_ref.shape[0], fn)


def _in_proj_body(x_ref, g_ref, w_ref, oa_ref, or_ref, xn_ref, *, n_a):
    j = pl.program_id(1)

    @pl.when(j == 0)
    def _():
        _norm_rows_to(xn_ref, x_ref, g_ref)

    @pl.when(j < n_a)
    def _():
        oa_ref[...] = jnp.dot(xn_ref[...], w_ref[...], preferred_element_type=_F32)

    @pl.when(j >= n_a)
    def _():
        or_ref[...] = jnp.dot(xn_ref[...], w_ref[...], preferred_element_type=_F32).astype(_BF16)


def _in_proj(x, g, w, a_cols):
    m, d = x.shape
    n = w.shape[1]
    tm = _pick_tile(m, (512, 256, 128))
    tn = next(t for t in (1024, 512, 256, 128) if a_cols % t == 0 and (n - a_cols) % t == 0)
    n_a = a_cols // tn
    return pl.pallas_call(
        functools.partial(_in_proj_body, n_a=n_a),
        grid=(m // tm, n // tn),
        in_specs=[
            pl.BlockSpec((tm, d), lambda i, j: (i, 0)),
            pl.BlockSpec((1, d), lambda i, j: (0, 0)),
            pl.BlockSpec((d, tn), lambda i, j: (0, j)),
        ],
        out_specs=[
            pl.BlockSpec((tm, tn), lambda i, j: (i, jnp.minimum(j, n_a - 1))),
            pl.BlockSpec((tm, tn), lambda i, j: (i, jnp.maximum(j - n_a, 0))),
        ],
        out_shape=[
            jax.ShapeDtypeStruct((m, a_cols), _F32),
            jax.ShapeDtypeStruct((m, n - a_cols), _BF16),
        ],
        scratch_shapes=[pltpu.VMEM((tm, d), _BF16)],
        compiler_params=_params(("parallel", "arbitrary")),
        name="in_proj",
    )(x, g.reshape(1, d), w)


def _bucket_table():
    half = NUM_BUCKETS // 2
    max_exact = half // 2
    i = np.arange(Q_TILE_A)[:, None]
    c = np.arange(K_TILE_A)[None, :]
    out = np.zeros((len(A_BRANCHES), 3, Q_TILE_A, K_TILE_A), np.int32)
    for br, (_, dil) in enumerate(A_BRANCHES):
        for var in range(3):
            rel = c - SPAN * var - i
            dist = rel * dil
            n = np.abs(dist)
            t = np.log(np.maximum(n, 1) / max_exact) / math.log(MAX_DISTANCE / max_exact) * (half - max_exact)
            valid = np.abs(rel) <= SPAN
            frac = np.abs(t - np.round(t))
            assert np.all((frac > 1e-4) | (n <= max_exact) | (n >= MAX_DISTANCE) | ~valid), "bucket edge near an integer"
            large = np.minimum(max_exact + t.astype(np.int32), half - 1)
            bucket = np.where(dist > 0, half, 0) + np.where(n < max_exact, n, large)
            out[br, var] = np.where(valid, bucket, NUM_BUCKETS)
    return out


def _mixer_a_body(rel_ref, idx_ref, q_ref, k_ref, v_ref, o_ref, bias_ref, acc_ref, m_ref, l_ref, *, seq):
    h = pl.program_id(1)

    for br in range(len(A_BRANCHES)):
        for var in range(3):
            idx = idx_ref[br, var]

            def fill(bkt, bias, idx=idx):
                return jnp.where(idx == bkt, rel_ref[bkt, h], bias)

            bias_ref[br, var] = lax.fori_loop(0, NUM_BUCKETS, fill, jnp.full(idx.shape, NEG, _F32))

    for br, (_, dil) in enumerate(A_BRANCHES):
        length = seq // dil
        nblk = length // Q_TILE_A
        first = br == 0

        def rows(start, size, dil=dil):
            if dil == 1:
                return pl.ds(pl.multiple_of(start, SPAN), size)
            return pl.ds(start, size, stride=dil)

        def group(blocks, br=br, dil=dil, length=length, nblk=nblk, first=first, rows=rows):
            qsls, ss, vs = [], [], []
            for n, r in blocks:
                p0 = n * Q_TILE_A
                ks = jnp.clip(p0 - SPAN, 0, length - K_TILE_A)
                var = jnp.where(n == 0, 0, jnp.where(n == nblk - 1, 2, 1))
                qsl = rows(r + dil * p0, Q_TILE_A)
                ksl = rows(r + dil * ks, K_TILE_A)
                q = (q_ref[qsl, :] * Q_SCALE).astype(_BF16)
                k = k_ref[ksl, :].astype(_BF16)
                s = lax.dot_general(q, k, (((1,), (1,)), ((), ())), preferred_element_type=_F32)
                qsls.append(qsl)
                ss.append(s + bias_ref[br, var])
                vs.append(v_ref[ksl, :].astype(_BF16))
            s = jnp.stack(ss)
            m_blk = jnp.max(s, axis=-1, keepdims=True)
            if first:
                m_new = m_blk
                p = jnp.exp(s - m_new)
            else:
                m_old = jnp.stack([m_ref[qsl, :] for qsl in qsls])
                m_new = jnp.maximum(m_old, m_blk)
                p = jnp.exp(s - jnp.concatenate([m_new] * (K_TILE_A // HEAD_DIM), axis=-1))
            l_blk = jnp.sum(p, axis=-1, keepdims=True)
            p = p.astype(_BF16)
            pv = jnp.stack([jnp.dot(p[j], vs[j], preferred_element_type=_F32) for j in range(GROUP_A)])
            if first:
                acc_new = pv
                l_new = jnp.broadcast_to(l_blk, pv.shape)
                m_new = jnp.broadcast_to(m_new, pv.shape)
            else:
                alpha = jnp.exp(m_old - m_new)
                acc_new = alpha * jnp.stack([acc_ref[qsl, :] for qsl in qsls]) + pv
                l_new = alpha * jnp.stack([l_ref[qsl, :] for qsl in qsls]) + l_blk
            for j, qsl in enumerate(qsls):
                acc_ref[qsl, :] = acc_new[j]
                l_ref[qsl, :] = l_new[j]
                m_ref[qsl, :] = m_new[j]

        if dil == 1:
            def body(n4, carry, group=group):
                group([(n4 * GROUP_A + j, 0) for j in range(GROUP_A)])
                return carry

            lax.fori_loop(0, nblk // GROUP_A, body, 0)
        else:
            def body(it, carry, group=group, nblk=nblk):
                rg, n = it // nblk, it % nblk
                group([(n, rg * GROUP_A + j) for j in range(GROUP_A)])
                return carry

            lax.fori_loop(0, (dil // GROUP_A) * nblk, body, 0)

    chunk = _pick_tile(seq, (512, 256, 128))

    def finish(c, carry):
        sl = pl.ds(pl.multiple_of(c * chunk, chunk), chunk)
        o_ref[sl, :] = (acc_ref[sl, :] / l_ref[sl, :]).astype(o_ref.dtype)
        return carry

    lax.fori_loop(0, seq // chunk, finish, 0)


def _mixer_a(proj_a, rel_bias, idx_tbl, batch, seq, heads):
    m = proj_a.shape[0]
    assert seq % (A_BRANCHES[-1][1] * K_TILE_A) == 0, "sequence too short for the widest dilation"
    assert all(dil == 1 or dil % GROUP_A == 0 for _, dil in A_BRANCHES) and (seq // Q_TILE_A) % GROUP_A == 0
    blk = (seq, HEAD_DIM)
    return pl.pallas_call(
        functools.partial(_mixer_a_body, seq=seq),
        grid=(batch, heads),
        in_specs=[
            pl.BlockSpec(memory_space=pltpu.SMEM),
            pl.BlockSpec(idx_tbl.shape, lambda b, h: (0, 0, 0, 0)),
            pl.BlockSpec(blk, lambda b, h: (b, h)),
            pl.BlockSpec(blk, lambda b, h: (b, heads + h)),
            pl.BlockSpec(blk, lambda b, h: (b, 2 * heads + h)),
        ],
        out_specs=pl.BlockSpec(blk, lambda b, h: (b, h)),
        out_shape=jax.ShapeDtypeStruct((m, heads * HEAD_DIM), _BF16),
        scratch_shapes=[
            pltpu.VMEM(idx_tbl.shape, _F32),
            pltpu.VMEM(blk, _F32),
            pltpu.VMEM(blk, _F32),
            pltpu.VMEM(blk, _F32),
        ],
        compiler_params=_params(("parallel", "parallel")),
        name="mixer_a",
    )(rel_bias, idx_tbl, proj_a, proj_a, proj_a)


def _mixer_b_body(gb_ref, gc_ref, hb_ref, gcp_ref, hbp_ref, gcn_ref, hbn_ref, w_ref, g_ref, o_ref, *, tiles_per_seq):
    i = pl.program_id(0)
    ts = gb_ref.shape[0]
    pos = i % tiles_per_seq
    u = gc_ref[...].astype(_F32) * hb_ref[...].astype(_F32)
    last = BF16_SUBLANE_TILE - 1
    u_prev = gcp_ref[last:last + 1, :].astype(_F32) * hbp_ref[last:last + 1, :].astype(_F32)
    u_next = gcn_ref[0:1, :].astype(_F32) * hbn_ref[0:1, :].astype(_F32)
    u_prev = jnp.where(pos == 0, 0.0, u_prev)
    u_next = jnp.where(pos == tiles_per_seq - 1, 0.0, u_next)
    row = lax.broadcasted_iota(jnp.int32, (ts, 1), 0)
    up = jnp.where(row == 0, u_prev, pltpu.roll(u, 1, 0))
    un = jnp.where(row == ts - 1, u_next, pltpu.roll(u, ts - 1, 0))
    y = gb_ref[...].astype(_F32) * (w_ref[0:1, :] * up + w_ref[1:2, :] * u + w_ref[2:3, :] * un)
    o_ref[...] = (y * _row_rms_inv(y) * g_ref[...]).astype(o_ref.dtype)


def _mixer_b(proj_r, conv_w, gain, seq):
    m = proj_r.shape[0]
    wb = gain.shape[0]
    ts = _pick_tile(seq, (512, 256, 128))
    halo = BF16_SUBLANE_TILE
    per = ts // halo
    nhalo = m // halo
    main = lambda c: pl.BlockSpec((ts, wb), lambda i: (i, c))
    prev = lambda c: pl.BlockSpec((halo, wb), lambda i: (jnp.maximum(i * per - 1, 0), c))
    nxt = lambda c: pl.BlockSpec((halo, wb), lambda i: (jnp.minimum((i + 1) * per, nhalo - 1), c))
    return pl.pallas_call(
        functools.partial(_mixer_b_body, tiles_per_seq=seq // ts),
        grid=(m // ts,),
        in_specs=[main(0), main(1), main(2), prev(1), prev(2), nxt(1), nxt(2),
                  pl.BlockSpec((CONV_WIDTH, wb), lambda i: (0, 0)),
                  pl.BlockSpec((1, wb), lambda i: (0, 0))],
        out_specs=pl.BlockSpec((ts, wb), lambda i: (i, 0)),
        out_shape=jax.ShapeDtypeStruct((m, wb), _BF16),
        compiler_params=_params(("parallel",)),
        name="mixer_b",
    )(proj_r, proj_r, proj_r, proj_r, proj_r, proj_r, proj_r, conv_w, gain.reshape(1, wb))


def _rope_tables(seq):
    quarter = HEAD_DIM // 4
    inv = ROPE_THETA ** (-jnp.arange(quarter, dtype=_F32) / quarter)
    t = jnp.arange(seq)
    ang_r = (t // GRID_W).astype(_F32)[:, None] * inv[None, :]
    ang_c = (t % GRID_W).astype(_F32)[:, None] * inv[None, :]
    zero = jnp.zeros_like(ang_r)
    cos = jnp.concatenate([jnp.cos(ang_r), jnp.cos(ang_r), jnp.cos(ang_c), jnp.cos(ang_c)], axis=-1)
    sin_hi = jnp.concatenate([-jnp.sin(ang_r), zero, -jnp.sin(ang_c), zero], axis=-1)
    sin_lo = jnp.concatenate([zero, jnp.sin(ang_r), zero, jnp.sin(ang_c)], axis=-1)
    return cos, sin_hi, sin_lo


def _rope_body(x_ref, g_ref, cos_ref, shi_ref, slo_ref, o_ref, *, q_heads):
    hh = pl.program_id(1)
    quarter = HEAD_DIM // 4
    xv = x_ref[...].astype(_F32)
    gain = jnp.where(hh < q_heads, g_ref[0:1, :], g_ref[1:2, :])
    xn = xv * _row_rms_inv(xv) * gain
    rot = (xn * cos_ref[...]
           + pltpu.roll(xn, HEAD_DIM - quarter, 1) * shi_ref[...]
           + pltpu.roll(xn, quarter, 1) * slo_ref[...])
    scale = jnp.where(hh < q_heads, Q_SCALE * LOG2_E, 1.0)
    o_ref[...] = (rot * scale).astype(o_ref.dtype)


def _rope_qk(proj_r, gains, tables, seq, col0, q_heads, kv_heads):
    m = proj_r.shape[0]
    ts = _pick_tile(seq, (2048, 1024, 512, 256, 128))
    per_seq = seq // ts
    blk0 = col0 // HEAD_DIM
    nh = q_heads + kv_heads
    tab = pl.BlockSpec((ts, HEAD_DIM), lambda i, hh: (i % per_seq, 0))
    return pl.pallas_call(
        functools.partial(_rope_body, q_heads=q_heads),
        grid=(m // ts, nh),
        in_specs=[pl.BlockSpec((ts, HEAD_DIM), lambda i, hh: (i, blk0 + hh)),
                  pl.BlockSpec((2, HEAD_DIM), lambda i, hh: (0, 0)),
                  tab, tab, tab],
        out_specs=pl.BlockSpec((ts, HEAD_DIM), lambda i, hh: (i, hh)),
        out_shape=jax.ShapeDtypeStruct((m, nh * HEAD_DIM), _BF16),
        compiler_params=_params(("parallel", "parallel")),
        name="rope_qk",
    )(proj_r, gains, *tables)


def _attn_c_body(q_ref, k_ref, v_ref, o_ref, vt_ref, qt_ref, acc_ref, *bufs, tk):
    nchunk = vt_ref.shape[0]
    tq = q_ref.shape[0]
    nq = GQA_GROUP * tq

    @pl.when(pl.program_id(2) == 0)
    def _():
        def transpose_v(c, carry):
            rows = pl.ds(pl.multiple_of(c * tk, tk), tk)
            vt_ref[c, 0:HEAD_DIM, :] = v_ref[rows, :].astype(_F32).T.astype(_BF16)
            vt_ref[c, HEAD_DIM:, :] = jnp.ones((BF16_SUBLANE_TILE, tk), _BF16)
            return carry

        lax.fori_loop(0, nchunk, transpose_v, 0)

    for g in range(GQA_GROUP):
        qt_ref[:, g * tq:(g + 1) * tq] = q_ref[:, g * HEAD_DIM:(g + 1) * HEAD_DIM].astype(_F32).T.astype(_BF16)
    acc_ref[...] = jnp.zeros(acc_ref.shape, _F32)

    s_sets = (bufs[0:2], bufs[2:4])
    p_sets = (bufs[4:6], bufs[6:8])
    npair = nchunk // 2

    def stage_scores(pair, s_set):
        out = []
        for j in range(2):
            rows = pl.ds(pl.multiple_of((2 * pair + j) * tk, tk), tk)
            s = jnp.dot(k_ref[rows, :], qt_ref[...], preferred_element_type=_F32)
            s_set[j][...] = s
            out.append(jnp.max(s, axis=0, keepdims=True))
        return tuple(out)

    def stage_exp(s_set, p_set, maxes, m_run):
        alphas = []
        for j in range(2):
            m_new = jnp.maximum(m_run, maxes[j])
            alphas.append(jnp.exp2(m_run - m_new))
            p_set[j][...] = jnp.exp2((s_set[j][...] - m_new).astype(_BF16))
            m_run = m_new
        return tuple(alphas), m_run

    def stage_pv(pair, p_set, alphas):
        for j in range(2):
            acc_ref[...] = alphas[j] * acc_ref[...] + jnp.dot(vt_ref[2 * pair + j], p_set[j][...],
                                                               preferred_element_type=_F32)

    m_run = jnp.full((1, nq), NEG, _F32)
    mx0 = stage_scores(0, s_sets[0])
    mx1 = stage_scores(1, s_sets[1])
    al0, m_run = stage_exp(s_sets[0], p_sets[0], mx0, m_run)

    def two_steps(it, carry):
        m_run, mx1, al0 = carry
        j = 2 * it
        mx0 = stage_scores(j + 2, s_sets[0])
        al1, m_run = stage_exp(s_sets[1], p_sets[1], mx1, m_run)
        stage_pv(j, p_sets[0], al0)
        mx1 = stage_scores(j + 3, s_sets[1])
        al0, m_run = stage_exp(s_sets[0], p_sets[0], mx0, m_run)
        stage_pv(j + 1, p_sets[1], al1)
        return m_run, mx1, al0

    m_run, mx1, al0 = lax.fori_loop(0, npair // 2 - 1, two_steps, (m_run, mx1, al0))
    al1, m_run = stage_exp(s_sets[1], p_sets[1], mx1, m_run)
    stage_pv(npair - 2, p_sets[0], al0)
    stage_pv(npair - 1, p_sets[1], al1)
    out_t = acc_ref[0:HEAD_DIM, :] / acc_ref[HEAD_DIM:HEAD_DIM + 1, :]
    for g in range(GQA_GROUP):
        o_ref[:, g * HEAD_DIM:(g + 1) * HEAD_DIM] = out_t[:, g * tq:(g + 1) * tq].T.astype(o_ref.dtype)


def _attn_c(qk, proj_r, batch, seq, q_heads, kv_heads, v_col0):
    m = qk.shape[0]
    assert q_heads == GQA_GROUP * kv_heads
    tq = _pick_tile(seq, (256, 128))
    tk = _pick_tile(seq, (512, 256, 128))
    assert (seq // tk) % 4 == 0, "the chunk pipeline advances two pairs of key chunks per loop step"
    nq = seq // tq
    gw = GQA_GROUP * HEAD_DIM
    v_blk0 = v_col0 // HEAD_DIM
    return pl.pallas_call(
        functools.partial(_attn_c_body, tk=tk),
        grid=(batch, kv_heads, nq),
        in_specs=[pl.BlockSpec((tq, gw), lambda b, g, i: (b * nq + i, g)),
                  pl.BlockSpec((seq, HEAD_DIM), lambda b, g, i: (b, q_heads + g)),
                  pl.BlockSpec((seq, HEAD_DIM), lambda b, g, i: (b, v_blk0 + g))],
        out_specs=pl.BlockSpec((tq, gw), lambda b, g, i: (b * nq + i, g)),
        out_shape=jax.ShapeDtypeStruct((m, q_heads * HEAD_DIM), _BF16),
        scratch_shapes=[pltpu.VMEM((seq // tk, HEAD_DIM + BF16_SUBLANE_TILE, tk), _BF16),
                        pltpu.VMEM((HEAD_DIM, GQA_GROUP * tq), _BF16),
                        pltpu.VMEM((HEAD_DIM + BF16_SUBLANE_TILE, GQA_GROUP * tq), _F32)]
        + [pltpu.VMEM((tk, GQA_GROUP * tq), _F32)] * 4
        + [pltpu.VMEM((tk, GQA_GROUP * tq), _BF16)] * 4,
        compiler_params=_params(("parallel", "parallel", "arbitrary")),
        name="attn_c",
    )(qk, qk, proj_r)


def _out_proj_body(oa_ref, ob_ref, oc_ref, ga_ref, gc_ref, w_ref, x_ref, gp_ref, o_ref, xn_ref, *, tk):
    k = pl.program_id(1)
    nk = pl.num_programs(1)

    @pl.when(k == 0)
    def _():
        def fn(rows):
            c0 = 0
            for ref, g_ref in ((oa_ref, ga_ref), (ob_ref, None), (oc_ref, gc_ref)):
                inv = None if g_ref is None else _rows_rms_inv(ref, rows)
                for cols in _lane_tiles(ref.shape[1]):
                    val = ref[rows, cols]
                    if g_ref is not None:
                        val = (val.astype(_F32) * inv * g_ref[:, cols]).astype(_BF16)
                    xn_ref[c0 + cols.start // tk, rows, cols.start % tk:cols.start % tk + LANES] = val
                c0 += ref.shape[1] // tk

        _for_row_chunks(oa_ref.shape[0], fn)
        o_ref[...] = jnp.dot(xn_ref[0], w_ref[...], preferred_element_type=_F32)

    @pl.when(k > 0)
    def _():
        o_ref[...] += jnp.dot(xn_ref[k], w_ref[...], preferred_element_type=_F32)

    @pl.when(k == nk - 1)
    def _():
        _residual_norm_rows(o_ref, x_ref, gp_ref)


def _out_proj(oa, ob, oc, ga, gc, w, x, gp):
    m, d = x.shape
    kdim = w.shape[0]
    wa, wb, wc = oa.shape[1], ob.shape[1], oc.shape[1]
    tm = _pick_tile(m, (512, 256, 128))
    tk = next(t for t in (512, 256, 128) if wa % t == 0 and wb % t == 0 and wc % t == 0)
    nk = kdim // tk
    row = lambda width: pl.BlockSpec((tm, width), lambda i, k: (i, 0))
    vec = lambda width: pl.BlockSpec((1, width), lambda i, k: (0, 0))
    return pl.pallas_call(
        functools.partial(_out_proj_body, tk=tk),
        grid=(m // tm, nk),
        in_specs=[row(wa), row(wb), row(wc), vec(wa), vec(wc),
                  pl.BlockSpec((tk, d), lambda i, k: (k, 0)),
                  row(d), vec(d)],
        out_specs=row(d),
        out_shape=jax.ShapeDtypeStruct((m, d), _F32),
        scratch_shapes=[pltpu.VMEM((nk, tm, tk), _BF16)],
        compiler_params=_params(("parallel", "arbitrary")),
        name="out_proj",
    )(oa, ob, oc, ga.reshape(1, wa), gc.reshape(1, wc), w, x, gp.reshape(1, d))


def _mlp_body(x_ref, g1_ref, wu_ref, wd_ref, g2_ref, o_ref, xn_ref, hid_ref):
    f = pl.program_id(1)
    nf = pl.num_programs(1)

    @pl.when(f == 0)
    def _():
        _norm_rows_to(xn_ref, x_ref, g1_ref)

    hid = jnp.dot(xn_ref[...], wu_ref[...], preferred_element_type=_F32)
    hid_ref[...] = jnp.square(jnp.maximum(hid, 0.0)).astype(_BF16)

    @pl.when(f == 0)
    def _():
        o_ref[...] = jnp.dot(hid_ref[...], wd_ref[...], preferred_element_type=_F32)

    @pl.when(f > 0)
    def _():
        o_ref[...] += jnp.dot(hid_ref[...], wd_ref[...], preferred_element_type=_F32)

    @pl.when(f == nf - 1)
    def _():
        _residual_norm_rows(o_ref, x_ref, g2_ref)


def _mlp(x, g1, wu, wd, g2):
    m, d = x.shape
    dff = wu.shape[1]
    tm = _pick_tile(m, (512, 256, 128))
    tf = _pick_tile(dff, (512, 256, 128))
    row = pl.BlockSpec((tm, d), lambda i, f: (i, 0))
    vec = pl.BlockSpec((1, d), lambda i, f: (0, 0))
    return pl.pallas_call(
        _mlp_body,
        grid=(m // tm, dff // tf),
        in_specs=[row, vec,
                  pl.BlockSpec((d, tf), lambda i, f: (0, f)),
                  pl.BlockSpec((tf, d), lambda i, f: (f, 0)),
                  vec],
        out_specs=row,
        out_shape=jax.ShapeDtypeStruct((m, d), _F32),
        scratch_shapes=[pltpu.VMEM((tm, d), _BF16), pltpu.VMEM((tm, tf), _BF16)],
        compiler_params=_params(("parallel", "arbitrary")),
        name="mlp",
    )(x, g1.reshape(1, d), wu, wd, g2.reshape(1, d))


def kernel(x, rel_bias, pre_mix_norm, w_in, conv_w, q_norm, k_norm, out_norm_a, out_norm_b, out_norm_c,
           w_out, post_mix_norm, pre_mlp_norm, w_up, w_down, post_mlp_norm):
    batch, seq, d = x.shape
    depth = w_in.shape[0]
    wa, wb, wc = out_norm_a.shape[1], out_norm_b.shape[1], out_norm_c.shape[1]
    in_width = w_in.shape[2]
    kv_width = (in_width - 3 * wa - 3 * wb - wc) // 2
    a_heads, q_heads, kv_heads = wa // HEAD_DIM, wc // HEAD_DIM, kv_width // HEAD_DIM
    assert all(w // (2 * dil) == SPAN for w, dil in A_BRANCHES)
    assert rel_bias.shape == (NUM_BUCKETS, a_heads) and seq % GRID_W == 0

    idx_tbl = jnp.asarray(_bucket_table())
    tables = _rope_tables(seq)
    xf = x.reshape(batch * seq, d)
    for i in range(depth):
        proj_a, proj_r = _in_proj(xf, pre_mix_norm[i], w_in[i].astype(_BF16), 3 * wa)
        oa = _mixer_a(proj_a, rel_bias, idx_tbl, batch, seq, a_heads)
        ob = _mixer_b(proj_r, conv_w[i], out_norm_b[i], seq)
        qk = _rope_qk(proj_r, jnp.stack([q_norm[i], k_norm[i]]), tables, seq, 3 * wb, q_heads, kv_heads)
        oc = _attn_c(qk, proj_r, batch, seq, q_heads, kv_heads, 3 * wb + wc + kv_width)
        xf = _out_proj(oa, ob, oc, out_norm_a[i], out_norm_c[i], w_out[i].astype(_BF16), xf, post_mix_norm[i])
        xf = _mlp(xf, pre_mlp_norm[i], w_up[i].astype(_BF16), w_down[i].astype(_BF16), post_mlp_norm[i])
    return xf.reshape(batch, seq, d)
```

```python
import functools
import math

import numpy as np
import jax
import jax.numpy as jnp
from jax import lax
from jax.experimental import pallas as pl
from jax.experimental.pallas import tpu as pltpu

HEAD_DIM = 128
A_BRANCHES = ((128, 1), (512, 4), (2048, 16))
SPAN = 64
Q_TILE_A = 2 * SPAN
K_TILE_A = 4 * SPAN
GROUP_A = 4
ROPE_THETA = 10000.0
GRID_W = 64
NUM_BUCKETS = 32
MAX_DISTANCE = 1024
CONV_WIDTH = 3
GQA_GROUP = 4
EPS = 1e-6
NEG = -1e30
Q_SCALE = HEAD_DIM ** -0.5
LOG2_E = math.log2(math.e)

V7X_VMEM_LIMIT_BYTES = 60 * 1024 * 1024
BF16_SUBLANE_TILE = 16

_F32 = jnp.float32
_BF16 = jnp.bfloat16


def _pick_tile(n, prefs):
    for t in prefs:
        if n % t == 0:
            return t
    return n


def _params(sem):
    return pltpu.CompilerParams(dimension_semantics=sem, vmem_limit_bytes=V7X_VMEM_LIMIT_BYTES)


def _row_rms_inv(v):
    return lax.rsqrt(jnp.mean(v * v, axis=-1, keepdims=True) + EPS)


ROW_CHUNK = 64
LANES = 128


def _for_row_chunks(nrows, fn):
    def step(c, carry):
        fn(pl.ds(pl.multiple_of(c * ROW_CHUNK, ROW_CHUNK), ROW_CHUNK))
        return carry

    lax.fori_loop(0, nrows // ROW_CHUNK, step, 0)


def _lane_tiles(width):
    return [slice(c, c + LANES) for c in range(0, width, LANES)]


def _rows_rms_inv(ref, rows):
    width = ref.shape[1]
    acc = None
    for cols in _lane_tiles(width):
        blk = ref[rows, cols].astype(_F32)
        acc = blk * blk if acc is None else acc + blk * blk
    inv = lax.rsqrt(jnp.sum(acc, axis=-1, keepdims=True) / width + EPS)
    return jnp.broadcast_to(inv, acc.shape)


def _norm_rows_to(dst_ref, src_ref, g_ref):
    def fn(rows):
        inv = _rows_rms_inv(src_ref, rows)
        for cols in _lane_tiles(src_ref.shape[1]):
            dst_ref[rows, cols] = (src_ref[rows, cols].astype(_F32) * inv * g_ref[:, cols]).astype(dst_ref.dtype)

    _for_row_chunks(src_ref.shape[0], fn)


def _residual_norm_rows(o_ref, x_ref, g_ref):
    def fn(rows):
        inv = _rows_rms_inv(o_ref, rows)
        for cols in _lane_tiles(o_ref.shape[1]):
            o_ref[rows, cols] = x_ref[rows, cols] + o_ref[rows, cols] * inv * g_ref[:, cols]

    _for_row_chunks(o_ref.shape[0], fn)


CAST_BLOCK_BYTES = 8 * 1024 * 1024


def _cast_body(w_ref, o_ref):
    o_ref[...] = w_ref[...].astype(o_ref.dtype)


def _cast_bf16(w):
    layers, k, n = w.shape
    tr = _pick_tile(k, [t for t in (2048, 1024, 512, 256, 128, 64, 32, 16) if t * n * 4 <= CAST_BLOCK_BYTES])
    spec = pl.BlockSpec((None, tr, n), lambda l, r: (l, r, 0))
    return pl.pallas_call(
        _cast_body,
        grid=(layers, k // tr),
        in_specs=[spec],
        out_specs=spec,
        out_shape=jax.ShapeDtypeStruct(w.shape, _BF16),
        compiler_params=_params(("parallel", "parallel")),
        name="cast_bf16",
    )(w)


def _in_proj_body(x_ref, g_ref, w_ref, oa_ref, or_ref, xn_ref, *, n_a):
    j = pl.program_id(1)

    @pl.when(j == 0)
    def _():
        _norm_rows_to(xn_ref, x_ref, g_ref)

    @pl.when(j < n_a)
    def _():
        oa_ref[...] = jnp.dot(xn_ref[...], w_ref[...], preferred_element_type=_F32)

    @pl.when(j >= n_a)
    def _():
        or_ref[...] = jnp.dot(xn_ref[...], w_ref[...], preferred_element_type=_F32).astype(_BF16)


def _in_proj(x, g, w, layer, a_cols):
    m, d = x.shape
    n = w.shape[2]
    tm = _pick_tile(m, (512, 256, 128))
    tn = next(t for t in (1024, 512, 256, 128) if a_cols % t == 0 and (n - a_cols) % t == 0)
    n_a = a_cols // tn
    return pl.pallas_call(
        functools.partial(_in_proj_body, n_a=n_a),
        grid=(m // tm, n // tn),
        in_specs=[
            pl.BlockSpec((tm, d), lambda i, j: (i, 0)),
            pl.BlockSpec((1, d), lambda i, j: (0, 0)),
            pl.BlockSpec((None, d, tn), lambda i, j: (layer, 0, j)),
        ],
        out_specs=[
            pl.BlockSpec((tm, tn), lambda i, j: (i, jnp.minimum(j, n_a - 1))),
            pl.BlockSpec((tm, tn), lambda i, j: (i, jnp.maximum(j - n_a, 0))),
        ],
        out_shape=[
            jax.ShapeDtypeStruct((m, a_cols), _F32),
            jax.ShapeDtypeStruct((m, n - a_cols), _BF16),
        ],
        scratch_shapes=[pltpu.VMEM((tm, d), _BF16)],
        compiler_params=_params(("parallel", "arbitrary")),
        name="in_proj",
    )(x, g.reshape(1, d), w)


def _bucket_table():
    half = NUM_BUCKETS // 2
    max_exact = half // 2
    i = np.arange(Q_TILE_A)[:, None]
    c = np.arange(K_TILE_A)[None, :]
    out = np.zeros((len(A_BRANCHES), 3, Q_TILE_A, K_TILE_A), np.int32)
    for br, (_, dil) in enumerate(A_BRANCHES):
        for var in range(3):
            rel = c - SPAN * var - i
            dist = rel * dil
            n = np.abs(dist)
            t = np.log(np.maximum(n, 1) / max_exact) / math.log(MAX_DISTANCE / max_exact) * (half - max_exact)
            valid = np.abs(rel) <= SPAN
            frac = np.abs(t - np.round(t))
            assert np.all((frac > 1e-4) | (n <= max_exact) | (n >= MAX_DISTANCE) | ~valid), "bucket edge near an integer"
            large = np.minimum(max_exact + t.astype(np.int32), half - 1)
            bucket = np.where(dist > 0, half, 0) + np.where(n < max_exact, n, large)
            out[br, var] = np.where(valid, bucket, NUM_BUCKETS)
    return out


def _mixer_a_body(rel_ref, idx_ref, q_ref, k_ref, v_ref, o_ref, bias_ref, acc_ref, m_ref, l_ref, *, seq):
    h = pl.program_id(1)

    for br in range(len(A_BRANCHES)):
        for var in range(3):
            idx = idx_ref[br, var]

            def fill(bkt, bias, idx=idx):
                return jnp.where(idx == bkt, rel_ref[bkt, h], bias)

            bias_ref[br, var] = lax.fori_loop(0, NUM_BUCKETS, fill, jnp.full(idx.shape, NEG, _F32))

    for br, (_, dil) in enumerate(A_BRANCHES):
        length = seq // dil
        nblk = length // Q_TILE_A
        first = br == 0

        def rows(start, size, dil=dil):
            if dil == 1:
                return pl.ds(pl.multiple_of(start, SPAN), size)
            return pl.ds(start, size, stride=dil)

        def group(blocks, br=br, dil=dil, length=length, nblk=nblk, first=first, rows=rows):
            qsls, ss, vs = [], [], []
            for n, r in blocks:
                p0 = n * Q_TILE_A
                ks = jnp.clip(p0 - SPAN, 0, length - K_TILE_A)
                var = jnp.where(n == 0, 0, jnp.where(n == nblk - 1, 2, 1))
                qsl = rows(r + dil * p0, Q_TILE_A)
                ksl = rows(r + dil * ks, K_TILE_A)
                q = (q_ref[qsl, :] * Q_SCALE).astype(_BF16)
                k = k_ref[ksl, :].astype(_BF16)
                s = lax.dot_general(q, k, (((1,), (1,)), ((), ())), preferred_element_type=_F32)
                qsls.append(qsl)
                ss.append(s + bias_ref[br, var])
                vs.append(v_ref[ksl, :].astype(_BF16))
            s = jnp.stack(ss)
            m_blk = jnp.max(s, axis=-1, keepdims=True)
            if first:
                m_new = m_blk
                p = jnp.exp(s - m_new)
            else:
                m_old = jnp.stack([m_ref[qsl, :] for qsl in qsls])
                m_new = jnp.maximum(m_old, m_blk)
                p = jnp.exp(s - jnp.concatenate([m_new] * (K_TILE_A // HEAD_DIM), axis=-1))
            l_blk = jnp.sum(p, axis=-1, keepdims=True)
            p = p.astype(_BF16)
            pv = jnp.stack([jnp.dot(p[j], vs[j], preferred_element_type=_F32) for j in range(GROUP_A)])
            if first:
                acc_new = pv
                l_new = jnp.broadcast_to(l_blk, pv.shape)
                m_new = jnp.broadcast_to(m_new, pv.shape)
            else:
                alpha = jnp.exp(m_old - m_new)
                acc_new = alpha * jnp.stack([acc_ref[qsl, :] for qsl in qsls]) + pv
                l_new = alpha * jnp.stack([l_ref[qsl, :] for qsl in qsls]) + l_blk
            for j, qsl in enumerate(qsls):
                acc_ref[qsl, :] = acc_new[j]
                l_ref[qsl, :] = l_new[j]
                m_ref[qsl, :] = m_new[j]

        if dil == 1:
            def body(n4, carry, group=group):
                group([(n4 * GROUP_A + j, 0) for j in range(GROUP_A)])
                return carry

            lax.fori_loop(0, nblk // GROUP_A, body, 0)
        else:
            def body(it, carry, group=group, nblk=nblk):
                rg, n = it // nblk, it % nblk
                group([(n, rg * GROUP_A + j) for j in range(GROUP_A)])
                return carry

            lax.fori_loop(0, (dil // GROUP_A) * nblk, body, 0)

    chunk = _pick_tile(seq, (512, 256, 128))

    def finish(c, carry):
        sl = pl.ds(pl.multiple_of(c * chunk, chunk), chunk)
        o_ref[sl, :] = (acc_ref[sl, :] / l_ref[sl, :]).astype(o_ref.dtype)
        return carry

    lax.fori_loop(0, seq // chunk, finish, 0)


def _mixer_a(proj_a, rel_bias, idx_tbl, batch, seq, heads):
    m = proj_a.shape[0]
    assert seq % (A_BRANCHES[-1][1] * K_TILE_A) == 0, "sequence too short for the widest dilation"
    assert all(dil == 1 or dil % GROUP_A == 0 for _, dil in A_BRANCHES) and (seq // Q_TILE_A) % GROUP_A == 0
    blk = (seq, HEAD_DIM)
    return pl.pallas_call(
        functools.partial(_mixer_a_body, seq=seq),
        grid=(batch, heads),
        in_specs=[
            pl.BlockSpec(memory_space=pltpu.SMEM),
            pl.BlockSpec(idx_tbl.shape, lambda b, h: (0, 0, 0, 0)),
            pl.BlockSpec(blk, lambda b, h: (b, h)),
            pl.BlockSpec(blk, lambda b, h: (b, heads + h)),
            pl.BlockSpec(blk, lambda b, h: (b, 2 * heads + h)),
        ],
        out_specs=pl.BlockSpec(blk, lambda b, h: (b, h)),
        out_shape=jax.ShapeDtypeStruct((m, heads * HEAD_DIM), _BF16),
        scratch_shapes=[
            pltpu.VMEM(idx_tbl.shape, _F32),
            pltpu.VMEM(blk, _F32),
            pltpu.VMEM(blk, _F32),
            pltpu.VMEM(blk, _F32),
        ],
        compiler_params=_params(("parallel", "parallel")),
        name="mixer_a",
    )(rel_bias, idx_tbl, proj_a, proj_a, proj_a)


def _mixer_b_body(gb_ref, gc_ref, hb_ref, gcp_ref, hbp_ref, gcn_ref, hbn_ref, w_ref, g_ref, o_ref, *, tiles_per_seq):
    i = pl.program_id(0)
    ts = gb_ref.shape[0]
    pos = i % tiles_per_seq
    u = gc_ref[...].astype(_F32) * hb_ref[...].astype(_F32)
    last = BF16_SUBLANE_TILE - 1
    u_prev = gcp_ref[last:last + 1, :].astype(_F32) * hbp_ref[last:last + 1, :].astype(_F32)
    u_next = gcn_ref[0:1, :].astype(_F32) * hbn_ref[0:1, :].astype(_F32)
    u_prev = jnp.where(pos == 0, 0.0, u_prev)
    u_next = jnp.where(pos == tiles_per_seq - 1, 0.0, u_next)
    row = lax.broadcasted_iota(jnp.int32, (ts, 1), 0)
    up = jnp.where(row == 0, u_prev, pltpu.roll(u, 1, 0))
    un = jnp.where(row == ts - 1, u_next, pltpu.roll(u, ts - 1, 0))
    y = gb_ref[...].astype(_F32) * (w_ref[0:1, :] * up + w_ref[1:2, :] * u + w_ref[2:3, :] * un)
    o_ref[...] = (y * _row_rms_inv(y) * g_ref[...]).astype(o_ref.dtype)


def _mixer_b(proj_r, conv_w, gain, seq):
    m = proj_r.shape[0]
    wb = gain.shape[0]
    ts = _pick_tile(seq, (512, 256, 128))
    halo = BF16_SUBLANE_TILE
    per = ts // halo
    nhalo = m // halo
    main = lambda c: pl.BlockSpec((ts, wb), lambda i: (i, c))
    prev = lambda c: pl.BlockSpec((halo, wb), lambda i: (jnp.maximum(i * per - 1, 0), c))
    nxt = lambda c: pl.BlockSpec((halo, wb), lambda i: (jnp.minimum((i + 1) * per, nhalo - 1), c))
    return pl.pallas_call(
        functools.partial(_mixer_b_body, tiles_per_seq=seq // ts),
        grid=(m // ts,),
        in_specs=[main(0), main(1), main(2), prev(1), prev(2), nxt(1), nxt(2),
                  pl.BlockSpec((CONV_WIDTH, wb), lambda i: (0, 0)),
                  pl.BlockSpec((1, wb), lambda i: (0, 0))],
        out_specs=pl.BlockSpec((ts, wb), lambda i: (i, 0)),
        out_shape=jax.ShapeDtypeStruct((m, wb), _BF16),
        compiler_params=_params(("parallel",)),
        name="mixer_b",
    )(proj_r, proj_r, proj_r, proj_r, proj_r, proj_r, proj_r, conv_w, gain.reshape(1, wb))


def _rope_tables(seq):
    quarter = HEAD_DIM // 4
    inv = ROPE_THETA ** (-jnp.arange(quarter, dtype=_F32) / quarter)
    t = jnp.arange(seq)
    ang_r = (t // GRID_W).astype(_F32)[:, None] * inv[None, :]
    ang_c = (t % GRID_W).astype(_F32)[:, None] * inv[None, :]
    zero = jnp.zeros_like(ang_r)
    cos = jnp.concatenate([jnp.cos(ang_r), jnp.cos(ang_r), jnp.cos(ang_c), jnp.cos(ang_c)], axis=-1)
    sin_hi = jnp.concatenate([-jnp.sin(ang_r), zero, -jnp.sin(ang_c), zero], axis=-1)
    sin_lo = jnp.concatenate([zero, jnp.sin(ang_r), zero, jnp.sin(ang_c)], axis=-1)
    return cos, sin_hi, sin_lo


def _rope_body(x_ref, g_ref, cos_ref, shi_ref, slo_ref, o_ref, *, q_heads):
    hh = pl.program_id(1)
    quarter = HEAD_DIM // 4
    xv = x_ref[...].astype(_F32)
    gain = jnp.where(hh < q_heads, g_ref[0:1, :], g_ref[1:2, :])
    xn = xv * _row_rms_inv(xv) * gain
    rot = (xn * cos_ref[...]
           + pltpu.roll(xn, HEAD_DIM - quarter, 1) * shi_ref[...]
           + pltpu.roll(xn, quarter, 1) * slo_ref[...])
    scale = jnp.where(hh < q_heads, Q_SCALE * LOG2_E, 1.0)
    o_ref[...] = (rot * scale).astype(o_ref.dtype)


def _rope_qk(proj_r, gains, tables, seq, col0, q_heads, kv_heads):
    m = proj_r.shape[0]
    ts = _pick_tile(seq, (2048, 1024, 512, 256, 128))
    per_seq = seq // ts
    blk0 = col0 // HEAD_DIM
    nh = q_heads + kv_heads
    tab = pl.BlockSpec((ts, HEAD_DIM), lambda i, hh: (i % per_seq, 0))
    return pl.pallas_call(
        functools.partial(_rope_body, q_heads=q_heads),
        grid=(m // ts, nh),
        in_specs=[pl.BlockSpec((ts, HEAD_DIM), lambda i, hh: (i, blk0 + hh)),
                  pl.BlockSpec((2, HEAD_DIM), lambda i, hh: (0, 0)),
                  tab, tab, tab],
        out_specs=pl.BlockSpec((ts, HEAD_DIM), lambda i, hh: (i, hh)),
        out_shape=jax.ShapeDtypeStruct((m, nh * HEAD_DIM), _BF16),
        compiler_params=_params(("parallel", "parallel")),
        name="rope_qk",
    )(proj_r, gains, *tables)


def _attn_c_body(q_ref, k_ref, v_ref, o_ref, vt_ref, qt_ref, acc_ref, *bufs, tk):
    nchunk = vt_ref.shape[0]
    tq = q_ref.shape[0]
    nq = GQA_GROUP * tq

    @pl.when(pl.program_id(2) == 0)
    def _():
        def transpose_v(c, carry):
            rows = pl.ds(pl.multiple_of(c * tk, tk), tk)
            vt_ref[c, 0:HEAD_DIM, :] = v_ref[rows, :].astype(_F32).T.astype(_BF16)
            vt_ref[c, HEAD_DIM:, :] = jnp.ones((BF16_SUBLANE_TILE, tk), _BF16)
            return carry

        lax.fori_loop(0, nchunk, transpose_v, 0)

    for g in range(GQA_GROUP):
        qt_ref[:, g * tq:(g + 1) * tq] = q_ref[:, g * HEAD_DIM:(g + 1) * HEAD_DIM].astype(_F32).T.astype(_BF16)
    acc_ref[...] = jnp.zeros(acc_ref.shape, _F32)

    s_sets = (bufs[0:2], bufs[2:4])
    p_sets = (bufs[4:6], bufs[6:8])
    npair = nchunk // 2

    def stage_scores(pair, s_set):
        out = []
        for j in range(2):
            rows = pl.ds(pl.multiple_of((2 * pair + j) * tk, tk), tk)
            s = jnp.dot(k_ref[rows, :], qt_ref[...], preferred_element_type=_F32)
            s_set[j][...] = s
            out.append(jnp.max(s, axis=0, keepdims=True))
        return tuple(out)

    def stage_exp(s_set, p_set, maxes, m_run):
        alphas = []
        for j in range(2):
            m_new = jnp.maximum(m_run, maxes[j])
            alphas.append(jnp.exp2(m_run - m_new))
            p_set[j][...] = jnp.exp2((s_set[j][...] - m_new).astype(_BF16))
            m_run = m_new
        return tuple(alphas), m_run

    def stage_pv(pair, p_set, alphas):
        for j in range(2):
            acc_ref[...] = alphas[j] * acc_ref[...] + jnp.dot(vt_ref[2 * pair + j], p_set[j][...],
                                                               preferred_element_type=_F32)

    m_run = jnp.full((1, nq), NEG, _F32)
    mx0 = stage_scores(0, s_sets[0])
    mx1 = stage_scores(1, s_sets[1])
    al0, m_run = stage_exp(s_sets[0], p_sets[0], mx0, m_run)

    def two_steps(it, carry):
        m_run, mx1, al0 = carry
        j = 2 * it
        mx0 = stage_scores(j + 2, s_sets[0])
        al1, m_run = stage_exp(s_sets[1], p_sets[1], mx1, m_run)
        stage_pv(j, p_sets[0], al0)
        mx1 = stage_scores(j + 3, s_sets[1])
        al0, m_run = stage_exp(s_sets[0], p_sets[0], mx0, m_run)
        stage_pv(j + 1, p_sets[1], al1)
        return m_run, mx1, al0

    m_run, mx1, al0 = lax.fori_loop(0, npair // 2 - 1, two_steps, (m_run, mx1, al0))
    al1, m_run = stage_exp(s_sets[1], p_sets[1], mx1, m_run)
    stage_pv(npair - 2, p_sets[0], al0)
    stage_pv(npair - 1, p_sets[1], al1)
    out_t = acc_ref[0:HEAD_DIM, :] / acc_ref[HEAD_DIM:HEAD_DIM + 1, :]
    for g in range(GQA_GROUP):
        o_ref[:, g * HEAD_DIM:(g + 1) * HEAD_DIM] = out_t[:, g * tq:(g + 1) * tq].T.astype(o_ref.dtype)


def _attn_c(qk, proj_r, batch, seq, q_heads, kv_heads, v_col0):
    m = qk.shape[0]
    assert q_heads == GQA_GROUP * kv_heads
    tq = _pick_tile(seq, (256, 128))
    tk = _pick_tile(seq, (512, 256, 128))
    assert (seq // tk) % 4 == 0, "the chunk pipeline advances two pairs of key chunks per loop step"
    nq = seq // tq
    gw = GQA_GROUP * HEAD_DIM
    v_blk0 = v_col0 // HEAD_DIM
    return pl.pallas_call(
        functools.partial(_attn_c_body, tk=tk),
        grid=(batch, kv_heads, nq),
        in_specs=[pl.BlockSpec((tq, gw), lambda b, g, i: (b * nq + i, g)),
                  pl.BlockSpec((seq, HEAD_DIM), lambda b, g, i: (b, q_heads + g)),
                  pl.BlockSpec((seq, HEAD_DIM), lambda b, g, i: (b, v_blk0 + g))],
        out_specs=pl.BlockSpec((tq, gw), lambda b, g, i: (b * nq + i, g)),
        out_shape=jax.ShapeDtypeStruct((m, q_heads * HEAD_DIM), _BF16),
        scratch_shapes=[pltpu.VMEM((seq // tk, HEAD_DIM + BF16_SUBLANE_TILE, tk), _BF16),
                        pltpu.VMEM((HEAD_DIM, GQA_GROUP * tq), _BF16),
                        pltpu.VMEM((HEAD_DIM + BF16_SUBLANE_TILE, GQA_GROUP * tq), _F32)]
        + [pltpu.VMEM((tk, GQA_GROUP * tq), _F32)] * 4
        + [pltpu.VMEM((tk, GQA_GROUP * tq), _BF16)] * 4,
        compiler_params=_params(("parallel", "parallel", "arbitrary")),
        name="attn_c",
    )(qk, qk, proj_r)


def _out_proj_body(oa_ref, ob_ref, oc_ref, ga_ref, gc_ref, w_ref, x_ref, gp_ref, o_ref, xn_ref, *, tk):
    k = pl.program_id(1)
    nk = pl.num_programs(1)

    @pl.when(k == 0)
    def _():
        def fn(rows):
            c0 = 0
            for ref, g_ref in ((oa_ref, ga_ref), (ob_ref, None), (oc_ref, gc_ref)):
                inv = None if g_ref is None else _rows_rms_inv(ref, rows)
                for cols in _lane_tiles(ref.shape[1]):
                    val = ref[rows, cols]
                    if g_ref is not None:
                        val = (val.astype(_F32) * inv * g_ref[:, cols]).astype(_BF16)
                    xn_ref[c0 + cols.start // tk, rows, cols.start % tk:cols.start % tk + LANES] = val
                c0 += ref.shape[1] // tk

        _for_row_chunks(oa_ref.shape[0], fn)
        o_ref[...] = jnp.dot(xn_ref[0], w_ref[...], preferred_element_type=_F32)

    @pl.when(k > 0)
    def _():
        o_ref[...] += jnp.dot(xn_ref[k], w_ref[...], preferred_element_type=_F32)

    @pl.when(k == nk - 1)
    def _():
        _residual_norm_rows(o_ref, x_ref, gp_ref)


def _out_proj(oa, ob, oc, ga, gc, w, layer, x, gp):
    m, d = x.shape
    kdim = w.shape[1]
    wa, wb, wc = oa.shape[1], ob.shape[1], oc.shape[1]
    tm = _pick_tile(m, (512, 256, 128))
    tk = next(t for t in (512, 256, 128) if wa % t == 0 and wb % t == 0 and wc % t == 0)
    nk = kdim // tk
    row = lambda width: pl.BlockSpec((tm, width), lambda i, k: (i, 0))
    vec = lambda width: pl.BlockSpec((1, width), lambda i, k: (0, 0))
    return pl.pallas_call(
        functools.partial(_out_proj_body, tk=tk),
        grid=(m // tm, nk),
        in_specs=[row(wa), row(wb), row(wc), vec(wa), vec(wc),
                  pl.BlockSpec((None, tk, d), lambda i, k: (layer, k, 0)),
                  row(d), vec(d)],
        out_specs=row(d),
        out_shape=jax.ShapeDtypeStruct((m, d), _F32),
        scratch_shapes=[pltpu.VMEM((nk, tm, tk), _BF16)],
        compiler_params=_params(("parallel", "arbitrary")),
        name="out_proj",
    )(oa, ob, oc, ga.reshape(1, wa), gc.reshape(1, wc), w, x, gp.reshape(1, d))


def _mlp_body(x_ref, g1_ref, wu_ref, wd_ref, g2_ref, o_ref, xn_ref, hid_ref):
    f = pl.program_id(1)
    nf = pl.num_programs(1)

    @pl.when(f == 0)
    def _():
        _norm_rows_to(xn_ref, x_ref, g1_ref)

    hid = jnp.dot(xn_ref[...], wu_ref[...], preferred_element_type=_F32)
    hid_ref[...] = jnp.square(jnp.maximum(hid, 0.0)).astype(_BF16)

    @pl.when(f == 0)
    def _():
        o_ref[...] = jnp.dot(hid_ref[...], wd_ref[...], preferred_element_type=_F32)

    @pl.when(f > 0)
    def _():
        o_ref[...] += jnp.dot(hid_ref[...], wd_ref[...], preferred_element_type=_F32)

    @pl.when(f == nf - 1)
    def _():
        _residual_norm_rows(o_ref, x_ref, g2_ref)


def _mlp(x, g1, wu, wd, layer, g2):
    m, d = x.shape
    dff = wu.shape[2]
    tm = _pick_tile(m, (512, 256, 128))
    tf = _pick_tile(dff, (512, 256, 128))
    row = pl.BlockSpec((tm, d), lambda i, f: (i, 0))
    vec = pl.BlockSpec((1, d), lambda i, f: (0, 0))
    return pl.pallas_call(
        _mlp_body,
        grid=(m // tm, dff // tf),
        in_specs=[row, vec,
                  pl.BlockSpec((None, d, tf), lambda i, f: (layer, 0, f)),
                  pl.BlockSpec((None, tf, d), lambda i, f: (layer, f, 0)),
                  vec],
        out_specs=row,
        out_shape=jax.ShapeDtypeStruct((m, d), _F32),
        scratch_shapes=[pltpu.VMEM((tm, d), _BF16), pltpu.VMEM((tm, tf), _BF16)],
        compiler_params=_params(("parallel", "arbitrary")),
        name="mlp",
    )(x, g1.reshape(1, d), wu, wd, g2.reshape(1, d))


def kernel(x, rel_bias, pre_mix_norm, w_in, conv_w, q_norm, k_norm, out_norm_a, out_norm_b, out_norm_c,
           w_out, post_mix_norm, pre_mlp_norm, w_up, w_down, post_mlp_norm):
    batch, seq, d = x.shape
    depth = w_in.shape[0]
    wa, wb, wc = out_norm_a.shape[1], out_norm_b.shape[1], out_norm_c.shape[1]
    in_width = w_in.shape[2]
    kv_width = (in_width - 3 * wa - 3 * wb - wc) // 2
    a_heads, q_heads, kv_heads = wa // HEAD_DIM, wc // HEAD_DIM, kv_width // HEAD_DIM
    assert all(w // (2 * dil) == SPAN for w, dil in A_BRANCHES)
    assert rel_bias.shape == (NUM_BUCKETS, a_heads) and seq % GRID_W == 0

    idx_tbl = jnp.asarray(_bucket_table())
    tables = _rope_tables(seq)
    xf = x.reshape(batch * seq, d)
    w_in, w_out, w_up, w_down = (_cast_bf16(w) for w in (w_in, w_out, w_up, w_down))
    for i in range(depth):
        proj_a, proj_r = _in_proj(xf, pre_mix_norm[i], w_in, i, 3 * wa)
        oa = _mixer_a(proj_a, rel_bias, idx_tbl, batch, seq, a_heads)
        ob = _mixer_b(proj_r, conv_w[i], out_norm_b[i], seq)
        qk = _rope_qk(proj_r, jnp.stack([q_norm[i], k_norm[i]]), tables, seq, 3 * wb, q_heads, kv_heads)
        oc = _attn_c(qk, proj_r, batch, seq, q_heads, kv_heads, 3 * wb + wc + kv_width)
        xf = _out_proj(oa, ob, oc, out_norm_a[i], out_norm_c[i], w_out, i, xf, post_mix_norm[i])
        xf = _mlp(xf, pre_mlp_norm[i], w_up, w_down, i, post_mlp_norm[i])
    return xf.reshape(batch, seq, d)
```

```python
import functools
import math

import numpy as np
import jax
import jax.numpy as jnp
from jax import lax
from jax.experimental import pallas as pl
from jax.experimental.pallas import tpu as pltpu

HEAD_DIM = 128
A_BRANCHES = ((128, 1), (512, 4), (2048, 16))
SPAN = 64
Q_TILE_A = 2 * SPAN
K_TILE_A = 4 * SPAN
GROUP_A = 4
ROPE_THETA = 10000.0
GRID_W = 64
NUM_BUCKETS = 32
MAX_DISTANCE = 1024
CONV_WIDTH = 3
GQA_GROUP = 4
EPS = 1e-6
NEG = -1e30
Q_SCALE = HEAD_DIM ** -0.5
LOG2_E = math.log2(math.e)

V7X_VMEM_LIMIT_BYTES = 60 * 1024 * 1024
BF16_SUBLANE_TILE = 16

_F32 = jnp.float32
_BF16 = jnp.bfloat16


def _pick_tile(n, prefs):
    for t in prefs:
        if n % t == 0:
            return t
    return n


def _params(sem):
    return pltpu.CompilerParams(dimension_semantics=sem, vmem_limit_bytes=V7X_VMEM_LIMIT_BYTES)


def _row_rms_inv(v):
    return lax.rsqrt(jnp.mean(v * v, axis=-1, keepdims=True) + EPS)


ROW_CHUNK = 64
LANES = 128


def _for_row_chunks(nrows, fn):
    def step(c, carry):
        fn(pl.ds(pl.multiple_of(c * ROW_CHUNK, ROW_CHUNK), ROW_CHUNK))
        return carry

    lax.fori_loop(0, nrows // ROW_CHUNK, step, 0)


def _lane_tiles(width):
    return [slice(c, c + LANES) for c in range(0, width, LANES)]


def _rows_rms_inv(ref, rows):
    width = ref.shape[1]
    acc = None
    for cols in _lane_tiles(width):
        blk = ref[rows, cols].astype(_F32)
        acc = blk * blk if acc is None else acc + blk * blk
    inv = lax.rsqrt(jnp.sum(acc, axis=-1, keepdims=True) / width + EPS)
    return jnp.broadcast_to(inv, acc.shape)


def _norm_rows_to(dst_ref, src_ref, g_ref):
    def fn(rows):
        inv = _rows_rms_inv(src_ref, rows)
        for cols in _lane_tiles(src_ref.shape[1]):
            dst_ref[rows, cols] = (src_ref[rows, cols].astype(_F32) * inv * g_ref[:, cols]).astype(dst_ref.dtype)

    _for_row_chunks(src_ref.shape[0], fn)


def _residual_norm_rows(o_ref, x_ref, g_ref):
    def fn(rows):
        inv = _rows_rms_inv(o_ref, rows)
        for cols in _lane_tiles(o_ref.shape[1]):
            o_ref[rows, cols] = x_ref[rows, cols] + o_ref[rows, cols] * inv * g_ref[:, cols]

    _for_row_chunks(o_ref.shape[0], fn)


CAST_BLOCK_BYTES = 8 * 1024 * 1024


def _cast_body(w_ref, o_ref):
    o_ref[...] = w_ref[...].astype(o_ref.dtype)


def _cast_bf16(w):
    layers, k, n = w.shape
    tr = _pick_tile(k, [t for t in (2048, 1024, 512, 256, 128, 64, 32, 16) if t * n * 4 <= CAST_BLOCK_BYTES])
    spec = pl.BlockSpec((None, tr, n), lambda l, r: (l, r, 0))
    return pl.pallas_call(
        _cast_body,
        grid=(layers, k // tr),
        in_specs=[spec],
        out_specs=spec,
        out_shape=jax.ShapeDtypeStruct(w.shape, _BF16),
        compiler_params=_params(("parallel", "parallel")),
        name="cast_bf16",
    )(w)


def _in_proj_body(x_ref, g_ref, w_ref, oa_ref, or_ref, xn_ref, *, n_a):
    j = pl.program_id(1)

    @pl.when(j == 0)
    def _():
        _norm_rows_to(xn_ref, x_ref, g_ref)

    @pl.when(j < n_a)
    def _():
        oa_ref[...] = jnp.dot(xn_ref[...], w_ref[...], preferred_element_type=_F32)

    @pl.when(j >= n_a)
    def _():
        or_ref[...] = jnp.dot(xn_ref[...], w_ref[...], preferred_element_type=_F32).astype(_BF16)


def _in_proj(x, g, w, layer, a_cols):
    m, d = x.shape
    n = w.shape[2]
    tm = _pick_tile(m, (512, 256, 128))
    tn = next(t for t in (1024, 512, 256, 128) if a_cols % t == 0 and (n - a_cols) % t == 0)
    n_a = a_cols // tn
    return pl.pallas_call(
        functools.partial(_in_proj_body, n_a=n_a),
        grid=(m // tm, n // tn),
        in_specs=[
            pl.BlockSpec((tm, d), lambda i, j: (i, 0)),
            pl.BlockSpec((1, d), lambda i, j: (0, 0)),
            pl.BlockSpec((None, d, tn), lambda i, j: (layer, 0, j)),
        ],
        out_specs=[
            pl.BlockSpec((tm, tn), lambda i, j: (i, jnp.minimum(j, n_a - 1))),
            pl.BlockSpec((tm, tn), lambda i, j: (i, jnp.maximum(j - n_a, 0))),
        ],
        out_shape=[
            jax.ShapeDtypeStruct((m, a_cols), _F32),
            jax.ShapeDtypeStruct((m, n - a_cols), _BF16),
        ],
        scratch_shapes=[pltpu.VMEM((tm, d), _BF16)],
        compiler_params=_params(("parallel", "arbitrary")),
        name="in_proj",
    )(x, g.reshape(1, d), w)


def _bucket_table():
    half = NUM_BUCKETS // 2
    max_exact = half // 2
    i = np.arange(Q_TILE_A)[:, None]
    c = np.arange(K_TILE_A)[None, :]
    out = np.zeros((len(A_BRANCHES), 3, Q_TILE_A, K_TILE_A), np.int32)
    for br, (_, dil) in enumerate(A_BRANCHES):
        for var in range(3):
            rel = c - SPAN * var - i
            dist = rel * dil
            n = np.abs(dist)
            t = np.log(np.maximum(n, 1) / max_exact) / math.log(MAX_DISTANCE / max_exact) * (half - max_exact)
            valid = np.abs(rel) <= SPAN
            frac = np.abs(t - np.round(t))
            assert np.all((frac > 1e-4) | (n <= max_exact) | (n >= MAX_DISTANCE) | ~valid), "bucket edge near an integer"
            large = np.minimum(max_exact + t.astype(np.int32), half - 1)
            bucket = np.where(dist > 0, half, 0) + np.where(n < max_exact, n, large)
            out[br, var] = np.where(valid, bucket, NUM_BUCKETS)
    return out


def _mixer_a_body(rel_ref, idx_ref, q_ref, k_ref, v_ref, o_ref, bias_ref, acc_ref, m_ref, l_ref, *, seq):
    h = pl.program_id(1)

    for br in range(len(A_BRANCHES)):
        for var in range(3):
            idx = idx_ref[br, var]

            def fill(bkt, bias, idx=idx):
                return jnp.where(idx == bkt, rel_ref[bkt, h], bias)

            bias_ref[br, var] = lax.fori_loop(0, NUM_BUCKETS, fill, jnp.full(idx.shape, NEG, _F32))

    for br, (_, dil) in enumerate(A_BRANCHES):
        length = seq // dil
        nblk = length // Q_TILE_A
        first = br == 0

        def rows(start, size, dil=dil):
            if dil == 1:
                return pl.ds(pl.multiple_of(start, SPAN), size)
            return pl.ds(start, size, stride=dil)

        def group(blocks, br=br, dil=dil, length=length, nblk=nblk, first=first, rows=rows):
            qsls, ss, vs = [], [], []
            for n, r in blocks:
                p0 = n * Q_TILE_A
                ks = jnp.clip(p0 - SPAN, 0, length - K_TILE_A)
                var = jnp.where(n == 0, 0, jnp.where(n == nblk - 1, 2, 1))
                qsl = rows(r + dil * p0, Q_TILE_A)
                ksl = rows(r + dil * ks, K_TILE_A)
                q = (q_ref[qsl, :] * Q_SCALE).astype(_BF16)
                k = k_ref[ksl, :].astype(_BF16)
                s = lax.dot_general(q, k, (((1,), (1,)), ((), ())), preferred_element_type=_F32)
                qsls.append(qsl)
                ss.append(s + bias_ref[br, var])
                vs.append(v_ref[ksl, :].astype(_BF16))
            s = jnp.stack(ss)
            m_blk = jnp.max(s, axis=-1, keepdims=True)
            if first:
                m_new = m_blk
                p = jnp.exp(s - m_new)
            else:
                m_old = jnp.stack([m_ref[qsl, :] for qsl in qsls])
                m_new = jnp.maximum(m_old, m_blk)
                p = jnp.exp(s - jnp.concatenate([m_new] * (K_TILE_A // HEAD_DIM), axis=-1))
            l_blk = jnp.sum(p, axis=-1, keepdims=True)
            p = p.astype(_BF16)
            pv = jnp.stack([jnp.dot(p[j], vs[j], preferred_element_type=_F32) for j in range(GROUP_A)])
            if first:
                acc_new = pv
                l_new = jnp.broadcast_to(l_blk, pv.shape)
                m_new = jnp.broadcast_to(m_new, pv.shape)
            else:
                alpha = jnp.exp(m_old - m_new)
                acc_new = alpha * jnp.stack([acc_ref[qsl, :] for qsl in qsls]) + pv
                l_new = alpha * jnp.stack([l_ref[qsl, :] for qsl in qsls]) + l_blk
            for j, qsl in enumerate(qsls):
                acc_ref[qsl, :] = acc_new[j]
                l_ref[qsl, :] = l_new[j]
                m_ref[qsl, :] = m_new[j]

        if dil == 1:
            def body(n4, carry, group=group):
                group([(n4 * GROUP_A + j, 0) for j in range(GROUP_A)])
                return carry

            lax.fori_loop(0, nblk // GROUP_A, body, 0)
        else:
            def body(it, carry, group=group, nblk=nblk):
                rg, n = it // nblk, it % nblk
                group([(n, rg * GROUP_A + j) for j in range(GROUP_A)])
                return carry

            lax.fori_loop(0, (dil // GROUP_A) * nblk, body, 0)

    chunk = _pick_tile(seq, (512, 256, 128))

    def finish(c, carry):
        sl = pl.ds(pl.multiple_of(c * chunk, chunk), chunk)
        o_ref[sl, :] = (acc_ref[sl, :] / l_ref[sl, :]).astype(o_ref.dtype)
        return carry

    lax.fori_loop(0, seq // chunk, finish, 0)


def _mixer_a(proj_a, rel_bias, idx_tbl, batch, seq, heads):
    m = proj_a.shape[0]
    assert seq % (A_BRANCHES[-1][1] * K_TILE_A) == 0, "sequence too short for the widest dilation"
    assert all(dil == 1 or dil % GROUP_A == 0 for _, dil in A_BRANCHES) and (seq // Q_TILE_A) % GROUP_A == 0
    blk = (seq, HEAD_DIM)
    return pl.pallas_call(
        functools.partial(_mixer_a_body, seq=seq),
        grid=(batch, heads),
        in_specs=[
            pl.BlockSpec(memory_space=pltpu.SMEM),
            pl.BlockSpec(idx_tbl.shape, lambda b, h: (0, 0, 0, 0)),
            pl.BlockSpec(blk, lambda b, h: (b, h)),
            pl.BlockSpec(blk, lambda b, h: (b, heads + h)),
            pl.BlockSpec(blk, lambda b, h: (b, 2 * heads + h)),
        ],
        out_specs=pl.BlockSpec(blk, lambda b, h: (b, h)),
        out_shape=jax.ShapeDtypeStruct((m, heads * HEAD_DIM), _BF16),
        scratch_shapes=[
            pltpu.VMEM(idx_tbl.shape, _F32),
            pltpu.VMEM(blk, _F32),
            pltpu.VMEM(blk, _F32),
            pltpu.VMEM(blk, _F32),
        ],
        compiler_params=_params(("parallel", "parallel")),
        name="mixer_a",
    )(rel_bias, idx_tbl, proj_a, proj_a, proj_a)


def _mixer_b_body(gb_ref, gc_ref, hb_ref, gcp_ref, hbp_ref, gcn_ref, hbn_ref, w_ref, g_ref, o_ref, *, tiles_per_seq):
    i = pl.program_id(0)
    ts = gb_ref.shape[0]
    pos = i % tiles_per_seq
    u = gc_ref[...].astype(_F32) * hb_ref[...].astype(_F32)
    last = BF16_SUBLANE_TILE - 1
    u_prev = gcp_ref[last:last + 1, :].astype(_F32) * hbp_ref[last:last + 1, :].astype(_F32)
    u_next = gcn_ref[0:1, :].astype(_F32) * hbn_ref[0:1, :].astype(_F32)
    u_prev = jnp.where(pos == 0, 0.0, u_prev)
    u_next = jnp.where(pos == tiles_per_seq - 1, 0.0, u_next)
    row = lax.broadcasted_iota(jnp.int32, (ts, 1), 0)
    up = jnp.where(row == 0, u_prev, pltpu.roll(u, 1, 0))
    un = jnp.where(row == ts - 1, u_next, pltpu.roll(u, ts - 1, 0))
    y = gb_ref[...].astype(_F32) * (w_ref[0:1, :] * up + w_ref[1:2, :] * u + w_ref[2:3, :] * un)
    o_ref[...] = (y * _row_rms_inv(y) * g_ref[...]).astype(o_ref.dtype)


def _mixer_b(proj_r, conv_w, gain, seq):
    m = proj_r.shape[0]
    wb = gain.shape[0]
    ts = _pick_tile(seq, (512, 256, 128))
    halo = BF16_SUBLANE_TILE
    per = ts // halo
    nhalo = m // halo
    main = lambda c: pl.BlockSpec((ts, wb), lambda i: (i, c))
    prev = lambda c: pl.BlockSpec((halo, wb), lambda i: (jnp.maximum(i * per - 1, 0), c))
    nxt = lambda c: pl.BlockSpec((halo, wb), lambda i: (jnp.minimum((i + 1) * per, nhalo - 1), c))
    return pl.pallas_call(
        functools.partial(_mixer_b_body, tiles_per_seq=seq // ts),
        grid=(m // ts,),
        in_specs=[main(0), main(1), main(2), prev(1), prev(2), nxt(1), nxt(2),
                  pl.BlockSpec((CONV_WIDTH, wb), lambda i: (0, 0)),
                  pl.BlockSpec((1, wb), lambda i: (0, 0))],
        out_specs=pl.BlockSpec((ts, wb), lambda i: (i, 0)),
        out_shape=jax.ShapeDtypeStruct((m, wb), _BF16),
        compiler_params=_params(("parallel",)),
        name="mixer_b",
    )(proj_r, proj_r, proj_r, proj_r, proj_r, proj_r, proj_r, conv_w, gain.reshape(1, wb))


def _rope_tables(seq):
    quarter = HEAD_DIM // 4
    inv = ROPE_THETA ** (-jnp.arange(quarter, dtype=_F32) / quarter)
    t = jnp.arange(seq)
    ang_r = (t // GRID_W).astype(_F32)[:, None] * inv[None, :]
    ang_c = (t % GRID_W).astype(_F32)[:, None] * inv[None, :]
    zero = jnp.zeros_like(ang_r)
    cos = jnp.concatenate([jnp.cos(ang_r), jnp.cos(ang_r), jnp.cos(ang_c), jnp.cos(ang_c)], axis=-1)
    sin_hi = jnp.concatenate([-jnp.sin(ang_r), zero, -jnp.sin(ang_c), zero], axis=-1)
    sin_lo = jnp.concatenate([zero, jnp.sin(ang_r), zero, jnp.sin(ang_c)], axis=-1)
    return cos, sin_hi, sin_lo


def _rope_body(x_ref, g_ref, cos_ref, shi_ref, slo_ref, o_ref, *, q_heads):
    hh = pl.program_id(1)
    quarter = HEAD_DIM // 4
    xv = x_ref[...].astype(_F32)
    gain = jnp.where(hh < q_heads, g_ref[0:1, :], g_ref[1:2, :])
    xn = xv * _row_rms_inv(xv) * gain
    rot = (xn * cos_ref[...]
           + pltpu.roll(xn, HEAD_DIM - quarter, 1) * shi_ref[...]
           + pltpu.roll(xn, quarter, 1) * slo_ref[...])
    scale = jnp.where(hh < q_heads, Q_SCALE * LOG2_E, 1.0)
    o_ref[...] = (rot * scale).astype(o_ref.dtype)


def _rope_qk(proj_r, gains, tables, seq, col0, q_heads, kv_heads):
    m = proj_r.shape[0]
    ts = _pick_tile(seq, (2048, 1024, 512, 256, 128))
    per_seq = seq // ts
    blk0 = col0 // HEAD_DIM
    nh = q_heads + kv_heads
    tab = pl.BlockSpec((ts, HEAD_DIM), lambda i, hh: (i % per_seq, 0))
    return pl.pallas_call(
        functools.partial(_rope_body, q_heads=q_heads),
        grid=(m // ts, nh),
        in_specs=[pl.BlockSpec((ts, HEAD_DIM), lambda i, hh: (i, blk0 + hh)),
                  pl.BlockSpec((2, HEAD_DIM), lambda i, hh: (0, 0)),
                  tab, tab, tab],
        out_specs=pl.BlockSpec((ts, HEAD_DIM), lambda i, hh: (i, hh)),
        out_shape=jax.ShapeDtypeStruct((m, nh * HEAD_DIM), _BF16),
        compiler_params=_params(("parallel", "parallel")),
        name="rope_qk",
    )(proj_r, gains, *tables)


def _attn_c_body(q_ref, k_ref, v_ref, o_ref, vt_ref, qt_ref, acc_ref, *bufs, tk):
    nchunk = vt_ref.shape[0]
    ntile = qt_ref.shape[0]
    tq = q_ref.shape[0] // ntile
    nq = GQA_GROUP * tq

    @pl.when(pl.program_id(2) == 0)
    def _():
        def transpose_v(c, carry):
            rows = pl.ds(pl.multiple_of(c * tk, tk), tk)
            vt_ref[c, 0:HEAD_DIM, :] = v_ref[rows, :].astype(_F32).T.astype(_BF16)
            vt_ref[c, HEAD_DIM:, :] = jnp.ones((BF16_SUBLANE_TILE, tk), _BF16)
            return carry

        lax.fori_loop(0, nchunk, transpose_v, 0)

    for t in range(ntile):
        for g in range(GQA_GROUP):
            q = q_ref[t * tq:(t + 1) * tq, g * HEAD_DIM:(g + 1) * HEAD_DIM]
            qt_ref[t, :, g * tq:(g + 1) * tq] = q.astype(_F32).T.astype(_BF16)
    acc_ref[...] = jnp.zeros(acc_ref.shape, _F32)

    s_sets = (bufs[0:2], bufs[2:4])
    p_sets = (bufs[4:6], bufs[6:8])
    npair = nchunk // 2
    nitem = ntile * npair

    def stage_scores(item, s_set):
        tile, pair = item // npair, item % npair
        out = []
        for j in range(2):
            rows = pl.ds(pl.multiple_of((2 * pair + j) * tk, tk), tk)
            s = jnp.dot(k_ref[rows, :], qt_ref[tile], preferred_element_type=_F32)
            s_set[j][...] = s
            out.append(jnp.max(s, axis=0, keepdims=True))
        return tuple(out)

    def stage_exp(item, s_set, p_set, maxes, m_run):
        m_run = jnp.where(item % npair == 0, NEG, m_run)
        alphas = []
        for j in range(2):
            m_new = jnp.maximum(m_run, maxes[j])
            alphas.append(jnp.exp2(m_run - m_new))
            p_set[j][...] = jnp.exp2((s_set[j][...] - m_new).astype(_BF16))
            m_run = m_new
        return tuple(alphas), m_run

    def stage_pv(item, p_set, alphas):
        tile, pair = item // npair, item % npair
        for j in range(2):
            acc_ref[tile] = alphas[j] * acc_ref[tile] + jnp.dot(vt_ref[2 * pair + j], p_set[j][...],
                                                                 preferred_element_type=_F32)

    m_run = jnp.full((1, nq), NEG, _F32)
    mx0 = stage_scores(0, s_sets[0])
    mx1 = stage_scores(1, s_sets[1])
    al0, m_run = stage_exp(0, s_sets[0], p_sets[0], mx0, m_run)

    def two_steps(it, carry):
        m_run, mx1, al0 = carry
        j = 2 * it
        mx0 = stage_scores(j + 2, s_sets[0])
        al1, m_run = stage_exp(j + 1, s_sets[1], p_sets[1], mx1, m_run)
        stage_pv(j, p_sets[0], al0)
        mx1 = stage_scores(j + 3, s_sets[1])
        al0, m_run = stage_exp(j + 2, s_sets[0], p_sets[0], mx0, m_run)
        stage_pv(j + 1, p_sets[1], al1)
        return m_run, mx1, al0

    m_run, mx1, al0 = lax.fori_loop(0, nitem // 2 - 1, two_steps, (m_run, mx1, al0))
    al1, m_run = stage_exp(nitem - 1, s_sets[1], p_sets[1], mx1, m_run)
    stage_pv(nitem - 2, p_sets[0], al0)
    stage_pv(nitem - 1, p_sets[1], al1)
    for t in range(ntile):
        out_t = acc_ref[t, 0:HEAD_DIM, :] / acc_ref[t, HEAD_DIM:HEAD_DIM + 1, :]
        for g in range(GQA_GROUP):
            o_ref[t * tq:(t + 1) * tq, g * HEAD_DIM:(g + 1) * HEAD_DIM] = (
                out_t[:, g * tq:(g + 1) * tq].T.astype(o_ref.dtype))


def _attn_c(qk, proj_r, batch, seq, q_heads, kv_heads, v_col0):
    m = qk.shape[0]
    assert q_heads == GQA_GROUP * kv_heads
    tq = _pick_tile(seq, (256, 128))
    ntile = next(t for t in (4, 2, 1) if seq % (t * tq) == 0)
    tk = _pick_tile(seq, (512, 256, 128))
    assert (seq // tk) % 4 == 0, "the chunk pipeline advances two pairs of key chunks per loop step"
    tq_step = ntile * tq
    nq = seq // tq_step
    gw = GQA_GROUP * HEAD_DIM
    v_blk0 = v_col0 // HEAD_DIM
    return pl.pallas_call(
        functools.partial(_attn_c_body, tk=tk),
        grid=(batch, kv_heads, nq),
        in_specs=[pl.BlockSpec((tq_step, gw), lambda b, g, i: (b * nq + i, g)),
                  pl.BlockSpec((seq, HEAD_DIM), lambda b, g, i: (b, q_heads + g)),
                  pl.BlockSpec((seq, HEAD_DIM), lambda b, g, i: (b, v_blk0 + g))],
        out_specs=pl.BlockSpec((tq_step, gw), lambda b, g, i: (b * nq + i, g)),
        out_shape=jax.ShapeDtypeStruct((m, q_heads * HEAD_DIM), _BF16),
        scratch_shapes=[pltpu.VMEM((seq // tk, HEAD_DIM + BF16_SUBLANE_TILE, tk), _BF16),
                        pltpu.VMEM((ntile, HEAD_DIM, GQA_GROUP * tq), _BF16),
                        pltpu.VMEM((ntile, HEAD_DIM + BF16_SUBLANE_TILE, GQA_GROUP * tq), _F32)]
        + [pltpu.VMEM((tk, GQA_GROUP * tq), _F32)] * 4
        + [pltpu.VMEM((tk, GQA_GROUP * tq), _BF16)] * 4,
        compiler_params=_params(("parallel", "parallel", "arbitrary")),
        name="attn_c",
    )(qk, qk, proj_r)


def _out_proj_body(oa_ref, ob_ref, oc_ref, ga_ref, gc_ref, w_ref, x_ref, gp_ref, o_ref, xn_ref, *, tk):
    k = pl.program_id(1)
    nk = pl.num_programs(1)

    @pl.when(k == 0)
    def _():
        def fn(rows):
            c0 = 0
            for ref, g_ref in ((oa_ref, ga_ref), (ob_ref, None), (oc_ref, gc_ref)):
                inv = None if g_ref is None else _rows_rms_inv(ref, rows)
                for cols in _lane_tiles(ref.shape[1]):
                    val = ref[rows, cols]
                    if g_ref is not None:
                        val = (val.astype(_F32) * inv * g_ref[:, cols]).astype(_BF16)
                    xn_ref[c0 + cols.start // tk, rows, cols.start % tk:cols.start % tk + LANES] = val
                c0 += ref.shape[1] // tk

        _for_row_chunks(oa_ref.shape[0], fn)
        o_ref[...] = jnp.dot(xn_ref[0], w_ref[...], preferred_element_type=_F32)

    @pl.when(k > 0)
    def _():
        o_ref[...] += jnp.dot(xn_ref[k], w_ref[...], preferred_element_type=_F32)

    @pl.when(k == nk - 1)
    def _():
        _residual_norm_rows(o_ref, x_ref, gp_ref)


def _out_proj(oa, ob, oc, ga, gc, w, layer, x, gp):
    m, d = x.shape
    kdim = w.shape[1]
    wa, wb, wc = oa.shape[1], ob.shape[1], oc.shape[1]
    tm = _pick_tile(m, (512, 256, 128))
    tk = next(t for t in (512, 256, 128) if wa % t == 0 and wb % t == 0 and wc % t == 0)
    nk = kdim // tk
    row = lambda width: pl.BlockSpec((tm, width), lambda i, k: (i, 0))
    vec = lambda width: pl.BlockSpec((1, width), lambda i, k: (0, 0))
    return pl.pallas_call(
        functools.partial(_out_proj_body, tk=tk),
        grid=(m // tm, nk),
        in_specs=[row(wa), row(wb), row(wc), vec(wa), vec(wc),
                  pl.BlockSpec((None, tk, d), lambda i, k: (layer, k, 0)),
                  row(d), vec(d)],
        out_specs=row(d),
        out_shape=jax.ShapeDtypeStruct((m, d), _F32),
        scratch_shapes=[pltpu.VMEM((nk, tm, tk), _BF16)],
        compiler_params=_params(("parallel", "arbitrary")),
        name="out_proj",
    )(oa, ob, oc, ga.reshape(1, wa), gc.reshape(1, wc), w, x, gp.reshape(1, d))


def _mlp_body(x_ref, g1_ref, wu_ref, wd_ref, g2_ref, o_ref, xn_ref, hid0_ref, hid1_ref, *, nf):
    f = pl.program_id(1)
    hid_refs = (hid0_ref, hid1_ref)

    def up(dst_ref):
        hid = jnp.dot(xn_ref[...], wu_ref[...], preferred_element_type=_F32)
        dst_ref[...] = jnp.square(jnp.maximum(hid, 0.0)).astype(_BF16)

    def down(src_ref):
        o_ref[...] += jnp.dot(src_ref[...], wd_ref[...], preferred_element_type=_F32)

    @pl.when(f == 0)
    def _():
        _norm_rows_to(xn_ref, x_ref, g1_ref)
        o_ref[...] = jnp.zeros(o_ref.shape, _F32)
        up(hid_refs[0])

    for parity in range(2):
        @pl.when((f > 0) & (f < nf) & (f % 2 == parity))
        def _(parity=parity):
            up(hid_refs[parity])
            down(hid_refs[1 - parity])

    @pl.when(f == nf)
    def _():
        down(hid_refs[(nf - 1) % 2])
        _residual_norm_rows(o_ref, x_ref, g2_ref)


def _mlp(x, g1, wu, wd, layer, g2):
    m, d = x.shape
    dff = wu.shape[2]
    tm = _pick_tile(m, (512, 256, 128))
    tf = _pick_tile(dff, (512, 256, 128))
    row = pl.BlockSpec((tm, d), lambda i, f: (i, 0))
    vec = pl.BlockSpec((1, d), lambda i, f: (0, 0))
    nf = dff // tf
    return pl.pallas_call(
        functools.partial(_mlp_body, nf=nf),
        grid=(m // tm, nf + 1),
        in_specs=[row, vec,
                  pl.BlockSpec((None, d, tf), lambda i, f: (layer, 0, jnp.minimum(f, nf - 1))),
                  pl.BlockSpec((None, tf, d), lambda i, f: (layer, jnp.maximum(f - 1, 0), 0)),
                  vec],
        out_specs=row,
        out_shape=jax.ShapeDtypeStruct((m, d), _F32),
        scratch_shapes=[pltpu.VMEM((tm, d), _BF16), pltpu.VMEM((tm, tf), _BF16), pltpu.VMEM((tm, tf), _BF16)],
        compiler_params=_params(("parallel", "arbitrary")),
        name="mlp",
    )(x, g1.reshape(1, d), wu, wd, g2.reshape(1, d))


def kernel(x, rel_bias, pre_mix_norm, w_in, conv_w, q_norm, k_norm, out_norm_a, out_norm_b, out_norm_c,
           w_out, post_mix_norm, pre_mlp_norm, w_up, w_down, post_mlp_norm):
    batch, seq, d = x.shape
    depth = w_in.shape[0]
    wa, wb, wc = out_norm_a.shape[1], out_norm_b.shape[1], out_norm_c.shape[1]
    in_width = w_in.shape[2]
    kv_width = (in_width - 3 * wa - 3 * wb - wc) // 2
    a_heads, q_heads, kv_heads = wa // HEAD_DIM, wc // HEAD_DIM, kv_width // HEAD_DIM
    assert all(w // (2 * dil) == SPAN for w, dil in A_BRANCHES)
    assert rel_bias.shape == (NUM_BUCKETS, a_heads) and seq % GRID_W == 0

    idx_tbl = jnp.asarray(_bucket_table())
    tables = _rope_tables(seq)
    xf = x.reshape(batch * seq, d)
    w_in, w_out, w_up, w_down = (_cast_bf16(w) for w in (w_in, w_out, w_up, w_down))
    for i in range(depth):
        proj_a, proj_r = _in_proj(xf, pre_mix_norm[i], w_in, i, 3 * wa)
        oa = _mixer_a(proj_a, rel_bias, idx_tbl, batch, seq, a_heads)
        ob = _mixer_b(proj_r, conv_w[i], out_norm_b[i], seq)
        qk = _rope_qk(proj_r, jnp.stack([q_norm[i], k_norm[i]]), tables, seq, 3 * wb, q_heads, kv_heads)
        oc = _attn_c(qk, proj_r, batch, seq, q_heads, kv_heads, 3 * wb + wc + kv_width)
        xf = _out_proj(oa, ob, oc, out_norm_a[i], out_norm_c[i], w_out, i, xf, post_mix_norm[i])
        xf = _mlp(xf, pre_mlp_norm[i], w_up, w_down, i, post_mlp_norm[i])
    return xf.reshape(batch, seq, d)
```

```python
import functools
import math

import numpy as np
import jax
import jax.numpy as jnp
from jax import lax
from jax.experimental import pallas as pl
from jax.experimental.pallas import tpu as pltpu

HEAD_DIM = 128
A_BRANCHES = ((128, 1), (512, 4), (2048, 16))
SPAN = 64
Q_TILE_A = 2 * SPAN
K_TILE_A = 4 * SPAN
GROUP_A = 8
ROPE_THETA = 10000.0
GRID_W = 64
NUM_BUCKETS = 32
MAX_DISTANCE = 1024
CONV_WIDTH = 3
GQA_GROUP = 4
EPS = 1e-6
NEG = -1e30
Q_SCALE = HEAD_DIM ** -0.5
LOG2_E = math.log2(math.e)

V7X_VMEM_LIMIT_BYTES = 60 * 1024 * 1024
BF16_SUBLANE_TILE = 16

_F32 = jnp.float32
_BF16 = jnp.bfloat16


def _pick_tile(n, prefs):
    for t in prefs:
        if n % t == 0:
            return t
    return n


def _params(sem):
    return pltpu.CompilerParams(dimension_semantics=sem, vmem_limit_bytes=V7X_VMEM_LIMIT_BYTES)


def _row_rms_inv(v):
    return lax.rsqrt(jnp.mean(v * v, axis=-1, keepdims=True) + EPS)


ROW_CHUNK = 64
LANES = 128


def _for_row_chunks(nrows, fn):
    def step(c, carry):
        fn(pl.ds(pl.multiple_of(c * ROW_CHUNK, ROW_CHUNK), ROW_CHUNK))
        return carry

    lax.fori_loop(0, nrows // ROW_CHUNK, step, 0)


def _lane_tiles(width):
    return [slice(c, c + LANES) for c in range(0, width, LANES)]


def _rows_rms_inv(ref, rows):
    width = ref.shape[1]
    acc = None
    for cols in _lane_tiles(width):
        blk = ref[rows, cols].astype(_F32)
        acc = blk * blk if acc is None else acc + blk * blk
    inv = lax.rsqrt(jnp.sum(acc, axis=-1, keepdims=True) / width + EPS)
    return jnp.broadcast_to(inv, acc.shape)


def _norm_rows_to(dst_ref, src_ref, g_ref):
    def fn(rows):
        inv = _rows_rms_inv(src_ref, rows)
        for cols in _lane_tiles(src_ref.shape[1]):
            dst_ref[rows, cols] = (src_ref[rows, cols].astype(_F32) * inv * g_ref[:, cols]).astype(dst_ref.dtype)

    _for_row_chunks(src_ref.shape[0], fn)


def _residual_norm_rows(o_ref, x_ref, g_ref):
    def fn(rows):
        inv = _rows_rms_inv(o_ref, rows)
        for cols in _lane_tiles(o_ref.shape[1]):
            o_ref[rows, cols] = x_ref[rows, cols] + o_ref[rows, cols] * inv * g_ref[:, cols]

    _for_row_chunks(o_ref.shape[0], fn)


CAST_BLOCK_BYTES = 8 * 1024 * 1024


def _cast_body(w_ref, o_ref):
    o_ref[...] = w_ref[...].astype(o_ref.dtype)


def _cast_bf16(w):
    layers, k, n = w.shape
    tr = _pick_tile(k, [t for t in (2048, 1024, 512, 256, 128, 64, 32, 16) if t * n * 4 <= CAST_BLOCK_BYTES])
    spec = pl.BlockSpec((None, tr, n), lambda l, r: (l, r, 0))
    return pl.pallas_call(
        _cast_body,
        grid=(layers, k // tr),
        in_specs=[spec],
        out_specs=spec,
        out_shape=jax.ShapeDtypeStruct(w.shape, _BF16),
        compiler_params=_params(("parallel", "parallel")),
        name="cast_bf16",
    )(w)


def _rope_head(v, gain, cos, sin_hi, sin_lo, scale):
    quarter = HEAD_DIM // 4
    xn = v * (_row_rms_inv(v) * scale) * gain
    return xn * cos + pltpu.roll(xn, HEAD_DIM - quarter, 1) * sin_hi + pltpu.roll(xn, quarter, 1) * sin_lo


def _in_proj_body(x_ref, g_ref, w_ref, hg_ref, cos_ref, shi_ref, slo_ref, oa_ref, or_ref, xn_ref, *, n_a, kinds):
    j = pl.program_id(1)

    @pl.when(j == 0)
    def _():
        _norm_rows_to(xn_ref, x_ref, g_ref)

    @pl.when(j < n_a)
    def _():
        oa_ref[...] = jnp.dot(xn_ref[...], w_ref[...], preferred_element_type=_F32)

    for sig in sorted(set(kinds)):
        tiles = [t for t, k in enumerate(kinds) if k == sig]

        @pl.when(functools.reduce(jnp.logical_or, [j == n_a + t for t in tiles]))
        def _(sig=sig):
            if all(kind == "p" for kind in sig):
                or_ref[...] = jnp.dot(xn_ref[...], w_ref[...], preferred_element_type=_F32).astype(_BF16)
                return
            for s0 in range(0, len(sig), 2):
                pair = slice(s0 * HEAD_DIM, (s0 + 2) * HEAD_DIM)
                res = jnp.dot(xn_ref[...], w_ref[:, pair], preferred_element_type=_F32)
                for s in range(2):
                    v = res[:, s * HEAD_DIM:(s + 1) * HEAD_DIM]
                    kind = sig[s0 + s]
                    if kind == "q":
                        v = _rope_head(v, hg_ref[0:1, :], cos_ref[...], shi_ref[...], slo_ref[...], Q_SCALE * LOG2_E)
                    elif kind == "k":
                        v = _rope_head(v, hg_ref[1:2, :], cos_ref[...], shi_ref[...], slo_ref[...], 1.0)
                    or_ref[:, (s0 + s) * HEAD_DIM:(s0 + s + 1) * HEAD_DIM] = v.astype(_BF16)


def _in_proj(x, g, w, layer, a_cols, head_gains, tables, seq, q_cols, k_cols):
    m, d = x.shape
    n = w.shape[2]
    tm = _pick_tile(seq, (512, 256, 128))
    tn = next(t for t in (1024, 512, 256, 128) if a_cols % t == 0 and (n - a_cols) % t == 0)
    n_a = a_cols // tn

    def kind(col):
        return "q" if q_cols[0] <= col < q_cols[1] else "k" if k_cols[0] <= col < k_cols[1] else "p"

    kinds = tuple(tuple(kind(t * tn + s * HEAD_DIM) for s in range(tn // HEAD_DIM)) for t in range((n - a_cols) // tn))
    per_seq = seq // tm
    tab = pl.BlockSpec((tm, HEAD_DIM), lambda i, j: (i % per_seq, 0))
    return pl.pallas_call(
        functools.partial(_in_proj_body, n_a=n_a, kinds=kinds),
        grid=(m // tm, n // tn),
        in_specs=[
            pl.BlockSpec((tm, d), lambda i, j: (i, 0)),
            pl.BlockSpec((1, d), lambda i, j: (0, 0)),
            pl.BlockSpec((None, d, tn), lambda i, j: (layer, 0, j)),
            pl.BlockSpec((2, HEAD_DIM), lambda i, j: (0, 0)),
            tab, tab, tab,
        ],
        out_specs=[
            pl.BlockSpec((tm, tn), lambda i, j: (i, jnp.minimum(j, n_a - 1))),
            pl.BlockSpec((tm, tn), lambda i, j: (i, jnp.maximum(j - n_a, 0))),
        ],
        out_shape=[
            jax.ShapeDtypeStruct((m, a_cols), _F32),
            jax.ShapeDtypeStruct((m, n - a_cols), _BF16),
        ],
        scratch_shapes=[pltpu.VMEM((tm, d), _BF16)],
        compiler_params=_params(("parallel", "arbitrary")),
        name="in_proj",
    )(x, g.reshape(1, d), w, head_gains, *tables)


def _bucket_table():
    half = NUM_BUCKETS // 2
    max_exact = half // 2
    i = np.arange(Q_TILE_A)[:, None]
    c = np.arange(K_TILE_A)[None, :]
    out = np.zeros((len(A_BRANCHES), 3, Q_TILE_A, K_TILE_A), np.int32)
    for br, (_, dil) in enumerate(A_BRANCHES):
        for var in range(3):
            rel = c - SPAN * var - i
            dist = rel * dil
            n = np.abs(dist)
            t = np.log(np.maximum(n, 1) / max_exact) / math.log(MAX_DISTANCE / max_exact) * (half - max_exact)
            valid = np.abs(rel) <= SPAN
            frac = np.abs(t - np.round(t))
            assert np.all((frac > 1e-4) | (n <= max_exact) | (n >= MAX_DISTANCE) | ~valid), "bucket edge near an integer"
            large = np.minimum(max_exact + t.astype(np.int32), half - 1)
            bucket = np.where(dist > 0, half, 0) + np.where(n < max_exact, n, large)
            out[br, var] = np.where(valid, bucket, NUM_BUCKETS)
    return out


def _mixer_a_body(rel_ref, idx_ref, q_ref, k_ref, v_ref, o_ref, bias_ref, acc_ref, m_ref, l_ref, *, seq):
    h = pl.program_id(1)

    for br in range(len(A_BRANCHES)):
        for var in range(3):
            idx = idx_ref[br, var]

            def fill(bkt, bias, idx=idx):
                return jnp.where(idx == bkt, rel_ref[bkt, h], bias)

            bias_ref[br, var] = lax.fori_loop(0, NUM_BUCKETS, fill, jnp.full(idx.shape, NEG, _F32))

    for br, (_, dil) in enumerate(A_BRANCHES):
        length = seq // dil
        nblk = length // Q_TILE_A
        first = br == 0

        def rows(start, size, dil=dil):
            if dil == 1:
                return pl.ds(pl.multiple_of(start, SPAN), size)
            return pl.ds(start, size, stride=dil)

        def group(blocks, br=br, dil=dil, length=length, nblk=nblk, first=first, rows=rows):
            qsls, ss, vs = [], [], []
            for n, r in blocks:
                p0 = n * Q_TILE_A
                ks = jnp.clip(p0 - SPAN, 0, length - K_TILE_A)
                var = jnp.where(n == 0, 0, jnp.where(n == nblk - 1, 2, 1))
                qsl = rows(r + dil * p0, Q_TILE_A)
                ksl = rows(r + dil * ks, K_TILE_A)
                q = (q_ref[qsl, :] * Q_SCALE).astype(_BF16)
                k = k_ref[ksl, :].astype(_BF16)
                s = lax.dot_general(q, k, (((1,), (1,)), ((), ())), preferred_element_type=_F32)
                qsls.append(qsl)
                ss.append(s + bias_ref[br, var])
                vs.append(v_ref[ksl, :].astype(_BF16))
            s = jnp.stack(ss)
            m_blk = jnp.max(s, axis=-1, keepdims=True)
            if first:
                m_new = m_blk
                p = jnp.exp(s - m_new)
            else:
                m_old = jnp.stack([m_ref[qsl, :] for qsl in qsls])
                m_new = jnp.maximum(m_old, m_blk)
                p = jnp.exp(s - jnp.concatenate([m_new] * (K_TILE_A // HEAD_DIM), axis=-1))
            l_blk = jnp.sum(p, axis=-1, keepdims=True)
            p = p.astype(_BF16)
            pv = jnp.stack([jnp.dot(p[j], vs[j], preferred_element_type=_F32) for j in range(GROUP_A)])
            if first:
                acc_new = pv
                l_new = jnp.broadcast_to(l_blk, pv.shape)
                m_new = jnp.broadcast_to(m_new, pv.shape)
            else:
                alpha = jnp.exp(m_old - m_new)
                acc_new = alpha * jnp.stack([acc_ref[qsl, :] for qsl in qsls]) + pv
                l_new = alpha * jnp.stack([l_ref[qsl, :] for qsl in qsls]) + l_blk
            for j, qsl in enumerate(qsls):
                acc_ref[qsl, :] = acc_new[j]
                l_ref[qsl, :] = l_new[j]
                m_ref[qsl, :] = m_new[j]

        res_per_group = min(dil, GROUP_A)
        blk_per_group = GROUP_A // res_per_group
        ngrp_n = nblk // blk_per_group

        def body(it, carry, group=group, res_per_group=res_per_group, blk_per_group=blk_per_group, ngrp_n=ngrp_n):
            rg, ng = it // ngrp_n, it % ngrp_n
            group([(ng * blk_per_group + jn, rg * res_per_group + jr)
                   for jn in range(blk_per_group) for jr in range(res_per_group)])
            return carry

        lax.fori_loop(0, (dil // res_per_group) * ngrp_n, body, 0)

    chunk = _pick_tile(seq, (512, 256, 128))

    def finish(c, carry):
        sl = pl.ds(pl.multiple_of(c * chunk, chunk), chunk)
        o_ref[sl, :] = (acc_ref[sl, :] / l_ref[sl, :]).astype(o_ref.dtype)
        return carry

    lax.fori_loop(0, seq // chunk, finish, 0)


def _mixer_a(proj_a, rel_bias, idx_tbl, batch, seq, heads):
    m = proj_a.shape[0]
    assert seq % (A_BRANCHES[-1][1] * K_TILE_A) == 0, "sequence too short for the widest dilation"
    for _, dil in A_BRANCHES:
        assert GROUP_A % dil == 0 or dil % GROUP_A == 0
        assert (seq // (dil * Q_TILE_A)) % max(GROUP_A // dil, 1) == 0
    blk = (seq, HEAD_DIM)
    return pl.pallas_call(
        functools.partial(_mixer_a_body, seq=seq),
        grid=(batch, heads),
        in_specs=[
            pl.BlockSpec(memory_space=pltpu.SMEM),
            pl.BlockSpec(idx_tbl.shape, lambda b, h: (0, 0, 0, 0)),
            pl.BlockSpec(blk, lambda b, h: (b, h)),
            pl.BlockSpec(blk, lambda b, h: (b, heads + h)),
            pl.BlockSpec(blk, lambda b, h: (b, 2 * heads + h)),
        ],
        out_specs=pl.BlockSpec(blk, lambda b, h: (b, h)),
        out_shape=jax.ShapeDtypeStruct((m, heads * HEAD_DIM), _BF16),
        scratch_shapes=[
            pltpu.VMEM(idx_tbl.shape, _F32),
            pltpu.VMEM(blk, _F32),
            pltpu.VMEM(blk, _F32),
            pltpu.VMEM(blk, _F32),
        ],
        compiler_params=_params(("parallel", "parallel")),
        name="mixer_a",
    )(rel_bias, idx_tbl, proj_a, proj_a, proj_a)


def _mixer_b_body(gb_ref, gc_ref, hb_ref, gcp_ref, hbp_ref, gcn_ref, hbn_ref, w_ref, g_ref, o_ref, *, tiles_per_seq):
    i = pl.program_id(0)
    ts = gb_ref.shape[0]
    pos = i % tiles_per_seq
    u = gc_ref[...].astype(_F32) * hb_ref[...].astype(_F32)
    last = BF16_SUBLANE_TILE - 1
    u_prev = gcp_ref[last:last + 1, :].astype(_F32) * hbp_ref[last:last + 1, :].astype(_F32)
    u_next = gcn_ref[0:1, :].astype(_F32) * hbn_ref[0:1, :].astype(_F32)
    u_prev = jnp.where(pos == 0, 0.0, u_prev)
    u_next = jnp.where(pos == tiles_per_seq - 1, 0.0, u_next)
    row = lax.broadcasted_iota(jnp.int32, (ts, 1), 0)
    up = jnp.where(row == 0, u_prev, pltpu.roll(u, 1, 0))
    un = jnp.where(row == ts - 1, u_next, pltpu.roll(u, ts - 1, 0))
    y = gb_ref[...].astype(_F32) * (w_ref[0:1, :] * up + w_ref[1:2, :] * u + w_ref[2:3, :] * un)
    o_ref[...] = (y * _row_rms_inv(y) * g_ref[...]).astype(o_ref.dtype)


def _mixer_b(proj_r, conv_w, gain, seq):
    m = proj_r.shape[0]
    wb = gain.shape[0]
    ts = _pick_tile(seq, (512, 256, 128))
    halo = BF16_SUBLANE_TILE
    per = ts // halo
    nhalo = m // halo
    main = lambda c: pl.BlockSpec((ts, wb), lambda i: (i, c))
    prev = lambda c: pl.BlockSpec((halo, wb), lambda i: (jnp.maximum(i * per - 1, 0), c))
    nxt = lambda c: pl.BlockSpec((halo, wb), lambda i: (jnp.minimum((i + 1) * per, nhalo - 1), c))
    return pl.pallas_call(
        functools.partial(_mixer_b_body, tiles_per_seq=seq // ts),
        grid=(m // ts,),
        in_specs=[main(0), main(1), main(2), prev(1), prev(2), nxt(1), nxt(2),
                  pl.BlockSpec((CONV_WIDTH, wb), lambda i: (0, 0)),
                  pl.BlockSpec((1, wb), lambda i: (0, 0))],
        out_specs=pl.BlockSpec((ts, wb), lambda i: (i, 0)),
        out_shape=jax.ShapeDtypeStruct((m, wb), _BF16),
        compiler_params=_params(("parallel",)),
        name="mixer_b",
    )(proj_r, proj_r, proj_r, proj_r, proj_r, proj_r, proj_r, conv_w, gain.reshape(1, wb))


def _rope_tables(seq):
    quarter = HEAD_DIM // 4
    inv = ROPE_THETA ** (-jnp.arange(quarter, dtype=_F32) / quarter)
    t = jnp.arange(seq)
    ang_r = (t // GRID_W).astype(_F32)[:, None] * inv[None, :]
    ang_c = (t % GRID_W).astype(_F32)[:, None] * inv[None, :]
    zero = jnp.zeros_like(ang_r)
    cos = jnp.concatenate([jnp.cos(ang_r), jnp.cos(ang_r), jnp.cos(ang_c), jnp.cos(ang_c)], axis=-1)
    sin_hi = jnp.concatenate([-jnp.sin(ang_r), zero, -jnp.sin(ang_c), zero], axis=-1)
    sin_lo = jnp.concatenate([zero, jnp.sin(ang_r), zero, jnp.sin(ang_c)], axis=-1)
    return cos, sin_hi, sin_lo


def _attn_c_body(q_ref, k_ref, v_ref, o_ref, vt_ref, qt_ref, acc_ref, *bufs, tk):
    nchunk = vt_ref.shape[0]
    ntile = qt_ref.shape[0]
    tq = q_ref.shape[0] // ntile
    nq = GQA_GROUP * tq

    @pl.when(pl.program_id(2) == 0)
    def _():
        def transpose_v(c, carry):
            rows = pl.ds(pl.multiple_of(c * tk, tk), tk)
            vt_ref[c, 0:HEAD_DIM, :] = v_ref[rows, :].astype(_F32).T.astype(_BF16)
            vt_ref[c, HEAD_DIM:, :] = jnp.ones((BF16_SUBLANE_TILE, tk), _BF16)
            return carry

        lax.fori_loop(0, nchunk, transpose_v, 0)

    for t in range(ntile):
        for g in range(GQA_GROUP):
            q = q_ref[t * tq:(t + 1) * tq, g * HEAD_DIM:(g + 1) * HEAD_DIM]
            qt_ref[t, :, g * tq:(g + 1) * tq] = q.astype(_F32).T.astype(_BF16)
    acc_ref[...] = jnp.zeros(acc_ref.shape, _F32)

    s_sets = (bufs[0:2], bufs[2:4])
    p_sets = (bufs[4:6], bufs[6:8])
    npair = nchunk // 2
    nitem = ntile * npair

    def stage_scores(item, s_set):
        tile, pair = item // npair, item % npair
        out = []
        for j in range(2):
            rows = pl.ds(pl.multiple_of((2 * pair + j) * tk, tk), tk)
            s = jnp.dot(k_ref[rows, :], qt_ref[tile], preferred_element_type=_F32)
            s_set[j][...] = s
            out.append(jnp.max(s, axis=0, keepdims=True))
        return tuple(out)

    def stage_exp(item, s_set, p_set, maxes, m_run):
        m_run = jnp.where(item % npair == 0, NEG, m_run)
        alphas = []
        for j in range(2):
            m_new = jnp.maximum(m_run, maxes[j])
            alphas.append(jnp.exp2(m_run - m_new))
            p_set[j][...] = jnp.exp2((s_set[j][...] - m_new).astype(_BF16))
            m_run = m_new
        return tuple(alphas), m_run

    def stage_pv(item, p_set, alphas):
        tile, pair = item // npair, item % npair
        for j in range(2):
            acc_ref[tile] = alphas[j] * acc_ref[tile] + jnp.dot(vt_ref[2 * pair + j], p_set[j][...],
                                                                 preferred_element_type=_F32)

    m_run = jnp.full((1, nq), NEG, _F32)
    mx0 = stage_scores(0, s_sets[0])
    mx1 = stage_scores(1, s_sets[1])
    al0, m_run = stage_exp(0, s_sets[0], p_sets[0], mx0, m_run)

    def two_steps(it, carry):
        m_run, mx1, al0 = carry
        j = 2 * it
        mx0 = stage_scores(j + 2, s_sets[0])
        al1, m_run = stage_exp(j + 1, s_sets[1], p_sets[1], mx1, m_run)
        stage_pv(j, p_sets[0], al0)
        mx1 = stage_scores(j + 3, s_sets[1])
        al0, m_run = stage_exp(j + 2, s_sets[0], p_sets[0], mx0, m_run)
        stage_pv(j + 1, p_sets[1], al1)
        return m_run, mx1, al0

    m_run, mx1, al0 = lax.fori_loop(0, nitem // 2 - 1, two_steps, (m_run, mx1, al0))
    al1, m_run = stage_exp(nitem - 1, s_sets[1], p_sets[1], mx1, m_run)
    stage_pv(nitem - 2, p_sets[0], al0)
    stage_pv(nitem - 1, p_sets[1], al1)
    for t in range(ntile):
        out_t = acc_ref[t, 0:HEAD_DIM, :] / acc_ref[t, HEAD_DIM:HEAD_DIM + 1, :]
        for g in range(GQA_GROUP):
            o_ref[t * tq:(t + 1) * tq, g * HEAD_DIM:(g + 1) * HEAD_DIM] = (
                out_t[:, g * tq:(g + 1) * tq].T.astype(o_ref.dtype))


def _attn_c(proj_r, batch, seq, q_heads, kv_heads, q_col0):
    m = proj_r.shape[0]
    assert q_heads == GQA_GROUP * kv_heads and q_col0 % (GQA_GROUP * HEAD_DIM) == 0
    q_blk0 = q_col0 // (GQA_GROUP * HEAD_DIM)
    k_blk0 = q_col0 // HEAD_DIM + q_heads
    v_blk0 = k_blk0 + kv_heads
    tq = _pick_tile(seq, (256, 128))
    ntile = next(t for t in (4, 2, 1) if seq % (t * tq) == 0)
    tk = _pick_tile(seq, (512, 256, 128))
    assert (seq // tk) % 4 == 0, "the chunk pipeline advances two pairs of key chunks per loop step"
    tq_step = ntile * tq
    nq = seq // tq_step
    gw = GQA_GROUP * HEAD_DIM
    return pl.pallas_call(
        functools.partial(_attn_c_body, tk=tk),
        grid=(batch, kv_heads, nq),
        in_specs=[pl.BlockSpec((tq_step, gw), lambda b, g, i: (b * nq + i, q_blk0 + g)),
                  pl.BlockSpec((seq, HEAD_DIM), lambda b, g, i: (b, k_blk0 + g)),
                  pl.BlockSpec((seq, HEAD_DIM), lambda b, g, i: (b, v_blk0 + g))],
        out_specs=pl.BlockSpec((tq_step, gw), lambda b, g, i: (b * nq + i, g)),
        out_shape=jax.ShapeDtypeStruct((m, q_heads * HEAD_DIM), _BF16),
        scratch_shapes=[pltpu.VMEM((seq // tk, HEAD_DIM + BF16_SUBLANE_TILE, tk), _BF16),
                        pltpu.VMEM((ntile, HEAD_DIM, GQA_GROUP * tq), _BF16),
                        pltpu.VMEM((ntile, HEAD_DIM + BF16_SUBLANE_TILE, GQA_GROUP * tq), _F32)]
        + [pltpu.VMEM((tk, GQA_GROUP * tq), _F32)] * 4
        + [pltpu.VMEM((tk, GQA_GROUP * tq), _BF16)] * 4,
        compiler_params=_params(("parallel", "parallel", "arbitrary")),
        name="attn_c",
    )(proj_r, proj_r, proj_r)


def _out_proj_body(oa_ref, ob_ref, oc_ref, ga_ref, gc_ref, w_ref, x_ref, gp_ref, o_ref, xn_ref, *, tk):
    k = pl.program_id(1)
    nk = pl.num_programs(1)

    @pl.when(k == 0)
    def _():
        def fn(rows):
            c0 = 0
            for ref, g_ref in ((oa_ref, ga_ref), (ob_ref, None), (oc_ref, gc_ref)):
                inv = None if g_ref is None else _rows_rms_inv(ref, rows)
                for cols in _lane_tiles(ref.shape[1]):
                    val = ref[rows, cols]
                    if g_ref is not None:
                        val = (val.astype(_F32) * inv * g_ref[:, cols]).astype(_BF16)
                    xn_ref[c0 + cols.start // tk, rows, cols.start % tk:cols.start % tk + LANES] = val
                c0 += ref.shape[1] // tk

        _for_row_chunks(oa_ref.shape[0], fn)
        o_ref[...] = jnp.dot(xn_ref[0], w_ref[...], preferred_element_type=_F32)

    @pl.when(k > 0)
    def _():
        o_ref[...] += jnp.dot(xn_ref[k], w_ref[...], preferred_element_type=_F32)

    @pl.when(k == nk - 1)
    def _():
        _residual_norm_rows(o_ref, x_ref, gp_ref)


def _out_proj(oa, ob, oc, ga, gc, w, layer, x, gp):
    m, d = x.shape
    kdim = w.shape[1]
    wa, wb, wc = oa.shape[1], ob.shape[1], oc.shape[1]
    tm = _pick_tile(m, (512, 256, 128))
    tk = next(t for t in (512, 256, 128) if wa % t == 0 and wb % t == 0 and wc % t == 0)
    nk = kdim // tk
    row = lambda width: pl.BlockSpec((tm, width), lambda i, k: (i, 0))
    vec = lambda width: pl.BlockSpec((1, width), lambda i, k: (0, 0))
    return pl.pallas_call(
        functools.partial(_out_proj_body, tk=tk),
        grid=(m // tm, nk),
        in_specs=[row(wa), row(wb), row(wc), vec(wa), vec(wc),
                  pl.BlockSpec((None, tk, d), lambda i, k: (layer, k, 0)),
                  row(d), vec(d)],
        out_specs=row(d),
        out_shape=jax.ShapeDtypeStruct((m, d), _F32),
        scratch_shapes=[pltpu.VMEM((nk, tm, tk), _BF16)],
        compiler_params=_params(("parallel", "arbitrary")),
        name="out_proj",
    )(oa, ob, oc, ga.reshape(1, wa), gc.reshape(1, wc), w, x, gp.reshape(1, d))


def _mlp_body(x_ref, g1_ref, wu_ref, wd_ref, g2_ref, o_ref, xn_ref, hid0_ref, hid1_ref, *, nf):
    f = pl.program_id(1)
    hid_refs = (hid0_ref, hid1_ref)

    def up(dst_ref):
        hid = jnp.dot(xn_ref[...], wu_ref[...], preferred_element_type=_F32)
        dst_ref[...] = jnp.square(jnp.maximum(hid, 0.0)).astype(_BF16)

    def down(src_ref):
        o_ref[...] += jnp.dot(src_ref[...], wd_ref[...], preferred_element_type=_F32)

    @pl.when(f == 0)
    def _():
        _norm_rows_to(xn_ref, x_ref, g1_ref)
        o_ref[...] = jnp.zeros(o_ref.shape, _F32)
        up(hid_refs[0])

    for parity in range(2):
        @pl.when((f > 0) & (f < nf) & (f % 2 == parity))
        def _(parity=parity):
            up(hid_refs[parity])
            down(hid_refs[1 - parity])

    @pl.when(f == nf)
    def _():
        down(hid_refs[(nf - 1) % 2])
        _residual_norm_rows(o_ref, x_ref, g2_ref)


def _mlp(x, g1, wu, wd, layer, g2):
    m, d = x.shape
    dff = wu.shape[2]
    tm = _pick_tile(m, (512, 256, 128))
    tf = _pick_tile(dff, (512, 256, 128))
    row = pl.BlockSpec((tm, d), lambda i, f: (i, 0))
    vec = pl.BlockSpec((1, d), lambda i, f: (0, 0))
    nf = dff // tf
    return pl.pallas_call(
        functools.partial(_mlp_body, nf=nf),
        grid=(m // tm, nf + 1),
        in_specs=[row, vec,
                  pl.BlockSpec((None, d, tf), lambda i, f: (layer, 0, jnp.minimum(f, nf - 1))),
                  pl.BlockSpec((None, tf, d), lambda i, f: (layer, jnp.maximum(f - 1, 0), 0)),
                  vec],
        out_specs=row,
        out_shape=jax.ShapeDtypeStruct((m, d), _F32),
        scratch_shapes=[pltpu.VMEM((tm, d), _BF16), pltpu.VMEM((tm, tf), _BF16), pltpu.VMEM((tm, tf), _BF16)],
        compiler_params=_params(("parallel", "arbitrary")),
        name="mlp",
    )(x, g1.reshape(1, d), wu, wd, g2.reshape(1, d))


def kernel(x, rel_bias, pre_mix_norm, w_in, conv_w, q_norm, k_norm, out_norm_a, out_norm_b, out_norm_c,
           w_out, post_mix_norm, pre_mlp_norm, w_up, w_down, post_mlp_norm):
    batch, seq, d = x.shape
    depth = w_in.shape[0]
    wa, wb, wc = out_norm_a.shape[1], out_norm_b.shape[1], out_norm_c.shape[1]
    in_width = w_in.shape[2]
    kv_width = (in_width - 3 * wa - 3 * wb - wc) // 2
    a_heads, q_heads, kv_heads = wa // HEAD_DIM, wc // HEAD_DIM, kv_width // HEAD_DIM
    assert all(w // (2 * dil) == SPAN for w, dil in A_BRANCHES)
    assert rel_bias.shape == (NUM_BUCKETS, a_heads) and seq % GRID_W == 0

    idx_tbl = jnp.asarray(_bucket_table())
    tables = _rope_tables(seq)
    xf = x.reshape(batch * seq, d)
    w_in, w_out, w_up, w_down = (_cast_bf16(w) for w in (w_in, w_out, w_up, w_down))
    for i in range(depth):
        q0 = 3 * wb
        proj_a, proj_r = _in_proj(xf, pre_mix_norm[i], w_in, i, 3 * wa, jnp.stack([q_norm[i], k_norm[i]]), tables, seq,
                                  (q0, q0 + wc), (q0 + wc, q0 + wc + kv_width))
        oa = _mixer_a(proj_a, rel_bias, idx_tbl, batch, seq, a_heads)
        ob = _mixer_b(proj_r, conv_w[i], out_norm_b[i], seq)
        oc = _attn_c(proj_r, batch, seq, q_heads, kv_heads, q0)
        xf = _out_proj(oa, ob, oc, out_norm_a[i], out_norm_c[i], w_out, i, xf, post_mix_norm[i])
        xf = _mlp(xf, pre_mlp_norm[i], w_up, w_down, i, post_mlp_norm[i])
    return xf.reshape(batch, seq, d)
```

```python
import functools
import math

import numpy as np
import jax
import jax.numpy as jnp
from jax import lax
from jax.experimental import pallas as pl
from jax.experimental.pallas import tpu as pltpu

HEAD_DIM = 128
A_BRANCHES = ((128, 1), (512, 4), (2048, 16))
SPAN = 64
Q_TILE_A = 2 * SPAN
K_TILE_A = 4 * SPAN
GROUP_A = 8
ROPE_THETA = 10000.0
GRID_W = 64
NUM_BUCKETS = 32
MAX_DISTANCE = 1024
CONV_WIDTH = 3
GQA_GROUP = 4
EPS = 1e-6
NEG = -1e30
Q_SCALE = HEAD_DIM ** -0.5
LOG2_E = math.log2(math.e)

V7X_VMEM_LIMIT_BYTES = 60 * 1024 * 1024
BF16_SUBLANE_TILE = 16

_F32 = jnp.float32
_BF16 = jnp.bfloat16


def _pick_tile(n, prefs):
    for t in prefs:
        if n % t == 0:
            return t
    return n


def _params(sem):
    return pltpu.CompilerParams(dimension_semantics=sem, vmem_limit_bytes=V7X_VMEM_LIMIT_BYTES)


def _row_rms_inv(v):
    return lax.rsqrt(jnp.mean(v * v, axis=-1, keepdims=True) + EPS)


ROW_CHUNK = 64
LANES = 128


def _for_row_chunks(nrows, fn):
    def step(c, carry):
        fn(pl.ds(pl.multiple_of(c * ROW_CHUNK, ROW_CHUNK), ROW_CHUNK))
        return carry

    lax.fori_loop(0, nrows // ROW_CHUNK, step, 0)


def _lane_tiles(width):
    return [slice(c, c + LANES) for c in range(0, width, LANES)]


def _rows_rms_inv(ref, rows):
    width = ref.shape[1]
    acc = None
    for cols in _lane_tiles(width):
        blk = ref[rows, cols].astype(_F32)
        acc = blk * blk if acc is None else acc + blk * blk
    inv = lax.rsqrt(jnp.sum(acc, axis=-1, keepdims=True) / width + EPS)
    return jnp.broadcast_to(inv, acc.shape)


def _norm_rows_to(dst_ref, src_ref, g_ref):
    def fn(rows):
        inv = _rows_rms_inv(src_ref, rows)
        for cols in _lane_tiles(src_ref.shape[1]):
            dst_ref[rows, cols] = (src_ref[rows, cols].astype(_F32) * inv * g_ref[:, cols]).astype(dst_ref.dtype)

    _for_row_chunks(src_ref.shape[0], fn)


def _residual_norm_rows(o_ref, x_ref, g_ref):
    def fn(rows):
        inv = _rows_rms_inv(o_ref, rows)
        for cols in _lane_tiles(o_ref.shape[1]):
            o_ref[rows, cols] = x_ref[rows, cols] + o_ref[rows, cols] * inv * g_ref[:, cols]

    _for_row_chunks(o_ref.shape[0], fn)


CAST_BLOCK_BYTES = 8 * 1024 * 1024


def _cast_body(w_ref, o_ref):
    o_ref[...] = w_ref[...].astype(o_ref.dtype)


def _cast_bf16(w):
    layers, k, n = w.shape
    tr = _pick_tile(k, [t for t in (2048, 1024, 512, 256, 128, 64, 32, 16) if t * n * 4 <= CAST_BLOCK_BYTES])
    spec = pl.BlockSpec((None, tr, n), lambda l, r: (l, r, 0))
    return pl.pallas_call(
        _cast_body,
        grid=(layers, k // tr),
        in_specs=[spec],
        out_specs=spec,
        out_shape=jax.ShapeDtypeStruct(w.shape, _BF16),
        compiler_params=_params(("parallel", "parallel")),
        name="cast_bf16",
    )(w)


def _rope_head(v, gain, cos, sin_hi, sin_lo, scale):
    quarter = HEAD_DIM // 4
    lanes = v.ndim - 1
    xn = v * (_row_rms_inv(v) * scale) * gain
    return xn * cos + pltpu.roll(xn, HEAD_DIM - quarter, lanes) * sin_hi + pltpu.roll(xn, quarter, lanes) * sin_lo


def _in_proj_body(x_ref, g_ref, w_ref, hg_ref, cos_ref, shi_ref, slo_ref, oa_ref, or_ref, xn_ref, *, n_a, kinds):
    j = pl.program_id(1)

    @pl.when(j == 0)
    def _():
        _norm_rows_to(xn_ref, x_ref, g_ref)

    @pl.when(j < n_a)
    def _():
        oa_ref[...] = jnp.dot(xn_ref[...], w_ref[...], preferred_element_type=_F32)

    for sig in sorted(set(kinds)):
        tiles = [t for t, k in enumerate(kinds) if k == sig]

        @pl.when(functools.reduce(jnp.logical_or, [j == n_a + t for t in tiles]))
        def _(sig=sig):
            if all(kind == "p" for kind in sig):
                or_ref[...] = jnp.dot(xn_ref[...], w_ref[...], preferred_element_type=_F32).astype(_BF16)
                return
            res = jnp.dot(xn_ref[...], w_ref[...], preferred_element_type=_F32)
            head = lambda s: res[:, s * HEAD_DIM:(s + 1) * HEAD_DIM]
            for kind, gain_row, scale in (("q", 0, Q_SCALE * LOG2_E), ("k", 1, 1.0)):
                slots = [s for s, k in enumerate(sig) if k == kind]
                if slots:
                    v = jnp.stack([head(s) for s in slots])
                    v = _rope_head(v, hg_ref[gain_row:gain_row + 1, :], cos_ref[...], shi_ref[...], slo_ref[...], scale)
                    for n, s in enumerate(slots):
                        or_ref[:, s * HEAD_DIM:(s + 1) * HEAD_DIM] = v[n].astype(_BF16)
            for s, k in enumerate(sig):
                if k == "p":
                    or_ref[:, s * HEAD_DIM:(s + 1) * HEAD_DIM] = head(s).astype(_BF16)


def _in_proj(x, g, w, layer, a_cols, head_gains, tables, seq, q_cols, k_cols):
    m, d = x.shape
    n = w.shape[2]
    tm = _pick_tile(seq, (512, 256, 128))
    tn = next(t for t in (1024, 512, 256, 128) if a_cols % t == 0 and (n - a_cols) % t == 0)
    n_a = a_cols // tn

    def kind(col):
        return "q" if q_cols[0] <= col < q_cols[1] else "k" if k_cols[0] <= col < k_cols[1] else "p"

    kinds = tuple(tuple(kind(t * tn + s * HEAD_DIM) for s in range(tn // HEAD_DIM)) for t in range((n - a_cols) // tn))
    per_seq = seq // tm
    tab = pl.BlockSpec((tm, HEAD_DIM), lambda i, j: (i % per_seq, 0))
    return pl.pallas_call(
        functools.partial(_in_proj_body, n_a=n_a, kinds=kinds),
        grid=(m // tm, n // tn),
        in_specs=[
            pl.BlockSpec((tm, d), lambda i, j: (i, 0)),
            pl.BlockSpec((1, d), lambda i, j: (0, 0)),
            pl.BlockSpec((None, d, tn), lambda i, j: (layer, 0, j)),
            pl.BlockSpec((2, HEAD_DIM), lambda i, j: (0, 0)),
            tab, tab, tab,
        ],
        out_specs=[
            pl.BlockSpec((tm, tn), lambda i, j: (i, jnp.minimum(j, n_a - 1))),
            pl.BlockSpec((tm, tn), lambda i, j: (i, jnp.maximum(j - n_a, 0))),
        ],
        out_shape=[
            jax.ShapeDtypeStruct((m, a_cols), _F32),
            jax.ShapeDtypeStruct((m, n - a_cols), _BF16),
        ],
        scratch_shapes=[pltpu.VMEM((tm, d), _BF16)],
        compiler_params=_params(("parallel", "arbitrary")),
        name="in_proj",
    )(x, g.reshape(1, d), w, head_gains, *tables)


def _bucket_table():
    half = NUM_BUCKETS // 2
    max_exact = half // 2
    i = np.arange(Q_TILE_A)[:, None]
    c = np.arange(K_TILE_A)[None, :]
    out = np.zeros((len(A_BRANCHES), 3, Q_TILE_A, K_TILE_A), np.int32)
    for br, (_, dil) in enumerate(A_BRANCHES):
        for var in range(3):
            rel = c - SPAN * var - i
            dist = rel * dil
            n = np.abs(dist)
            t = np.log(np.maximum(n, 1) / max_exact) / math.log(MAX_DISTANCE / max_exact) * (half - max_exact)
            valid = np.abs(rel) <= SPAN
            frac = np.abs(t - np.round(t))
            assert np.all((frac > 1e-4) | (n <= max_exact) | (n >= MAX_DISTANCE) | ~valid), "bucket edge near an integer"
            large = np.minimum(max_exact + t.astype(np.int32), half - 1)
            bucket = np.where(dist > 0, half, 0) + np.where(n < max_exact, n, large)
            out[br, var] = np.where(valid, bucket, NUM_BUCKETS)
    return out


def _mixer_a_body(rel_ref, idx_ref, q_ref, k_ref, v_ref, o_ref, bias_ref, acc_ref, m_ref, l_ref, *, seq):
    h = pl.program_id(1)

    for br in range(len(A_BRANCHES)):
        for var in range(3):
            idx = idx_ref[br, var]

            def fill(bkt, bias, idx=idx):
                return jnp.where(idx == bkt, rel_ref[bkt, h], bias)

            bias_ref[br, var] = lax.fori_loop(0, NUM_BUCKETS, fill, jnp.full(idx.shape, NEG, _F32))

    order = sorted(range(len(A_BRANCHES)), key=lambda b: -A_BRANCHES[b][1])
    for br in order:
        dil = A_BRANCHES[br][1]
        length = seq // dil
        nblk = length // Q_TILE_A
        first = br == order[0]

        def rows(start, size, dil=dil):
            if dil == 1:
                return pl.ds(pl.multiple_of(start, SPAN), size)
            return pl.ds(start, size, stride=dil)

        def group(blocks, br=br, dil=dil, length=length, nblk=nblk, first=first, rows=rows):
            qsls, ss, vs = [], [], []
            whole = {}
            for n, r in blocks:
                p0 = n * Q_TILE_A
                qsl = rows(r + dil * p0, Q_TILE_A)
                q = (q_ref[qsl, :] * Q_SCALE).astype(_BF16)
                if isinstance(n, int):
                    ks = min(max(p0 - SPAN, 0), length - K_TILE_A)
                    var = 0 if n == 0 else 2 if n == nblk - 1 else 1
                    if id(r) not in whole:
                        seq_rows = rows(r, length)
                        whole[id(r)] = (k_ref[seq_rows, :].astype(_BF16), v_ref[seq_rows, :].astype(_BF16))
                    k, v = (t[ks:ks + K_TILE_A] for t in whole[id(r)])
                else:
                    ks = jnp.clip(p0 - SPAN, 0, length - K_TILE_A)
                    var = jnp.where(n == 0, 0, jnp.where(n == nblk - 1, 2, 1))
                    ksl = rows(r + dil * ks, K_TILE_A)
                    k, v = k_ref[ksl, :].astype(_BF16), v_ref[ksl, :].astype(_BF16)
                s = lax.dot_general(q, k, (((1,), (1,)), ((), ())), preferred_element_type=_F32)
                qsls.append(qsl)
                ss.append(s + bias_ref[br, var])
                vs.append(v)
            s = jnp.stack(ss)
            m_blk = jnp.max(s, axis=-1, keepdims=True)
            if first:
                m_new = m_blk
                p = jnp.exp(s - m_new)
            else:
                m_old = jnp.stack([m_ref[qsl, :] for qsl in qsls])
                m_new = jnp.maximum(m_old, m_blk)
                p = jnp.exp(s - jnp.concatenate([m_new] * (K_TILE_A // HEAD_DIM), axis=-1))
            l_blk = jnp.sum(p, axis=-1, keepdims=True)
            p = p.astype(_BF16)
            pv = jnp.stack([jnp.dot(p[j], vs[j], preferred_element_type=_F32) for j in range(GROUP_A)])
            if first:
                acc_new = pv
                l_new = jnp.broadcast_to(l_blk, pv.shape)
                m_new = jnp.broadcast_to(m_new, pv.shape)
            else:
                alpha = jnp.exp(m_old - m_new)
                acc_new = alpha * jnp.stack([acc_ref[qsl, :] for qsl in qsls]) + pv
                l_new = alpha * jnp.stack([l_ref[qsl, :] for qsl in qsls]) + l_blk
            for j, qsl in enumerate(qsls):
                acc_ref[qsl, :] = acc_new[j]
                l_ref[qsl, :] = l_new[j]
                m_ref[qsl, :] = m_new[j]

        if nblk <= GROUP_A:
            res_per_group = GROUP_A // nblk

            def body(rg, carry, group=group, res_per_group=res_per_group, nblk=nblk):
                residues = [rg * res_per_group + jr for jr in range(res_per_group)]
                group([(n, r) for r in residues for n in range(nblk)])
                return carry

            lax.fori_loop(0, dil // res_per_group, body, 0)
        else:
            res_per_group = min(dil, GROUP_A)
            blk_per_group = GROUP_A // res_per_group
            ngrp_n = nblk // blk_per_group

            def body(it, carry, group=group, res_per_group=res_per_group, blk_per_group=blk_per_group, ngrp_n=ngrp_n):
                rg, ng = it // ngrp_n, it % ngrp_n
                group([(ng * blk_per_group + jn, rg * res_per_group + jr)
                       for jn in range(blk_per_group) for jr in range(res_per_group)])
                return carry

            lax.fori_loop(0, (dil // res_per_group) * ngrp_n, body, 0)

    chunk = _pick_tile(seq, (512, 256, 128))

    def finish(c, carry):
        sl = pl.ds(pl.multiple_of(c * chunk, chunk), chunk)
        o_ref[sl, :] = (acc_ref[sl, :] / l_ref[sl, :]).astype(o_ref.dtype)
        return carry

    lax.fori_loop(0, seq // chunk, finish, 0)


def _mixer_a(proj_a, rel_bias, idx_tbl, batch, seq, heads):
    m = proj_a.shape[0]
    assert seq % (A_BRANCHES[-1][1] * K_TILE_A) == 0, "sequence too short for the widest dilation"
    for _, dil in A_BRANCHES:
        nblk = seq // (dil * Q_TILE_A)
        if nblk <= GROUP_A:
            assert GROUP_A % nblk == 0 and dil % (GROUP_A // nblk) == 0
        else:
            assert GROUP_A % min(dil, GROUP_A) == 0 and nblk % (GROUP_A // min(dil, GROUP_A)) == 0
    blk = (seq, HEAD_DIM)
    return pl.pallas_call(
        functools.partial(_mixer_a_body, seq=seq),
        grid=(batch, heads),
        in_specs=[
            pl.BlockSpec(memory_space=pltpu.SMEM),
            pl.BlockSpec(idx_tbl.shape, lambda b, h: (0, 0, 0, 0)),
            pl.BlockSpec(blk, lambda b, h: (b, h)),
            pl.BlockSpec(blk, lambda b, h: (b, heads + h)),
            pl.BlockSpec(blk, lambda b, h: (b, 2 * heads + h)),
        ],
        out_specs=pl.BlockSpec(blk, lambda b, h: (b, h)),
        out_shape=jax.ShapeDtypeStruct((m, heads * HEAD_DIM), _BF16),
        scratch_shapes=[
            pltpu.VMEM(idx_tbl.shape, _F32),
            pltpu.VMEM(blk, _F32),
            pltpu.VMEM(blk, _F32),
            pltpu.VMEM(blk, _F32),
        ],
        compiler_params=_params(("parallel", "parallel")),
        name="mixer_a",
    )(rel_bias, idx_tbl, proj_a, proj_a, proj_a)


def _mixer_b_body(gb_ref, gc_ref, hb_ref, gcp_ref, hbp_ref, gcn_ref, hbn_ref, w_ref, g_ref, o_ref, *, tiles_per_seq):
    i = pl.program_id(0)
    ts = gb_ref.shape[0]
    pos = i % tiles_per_seq
    u = gc_ref[...].astype(_F32) * hb_ref[...].astype(_F32)
    last = BF16_SUBLANE_TILE - 1
    u_prev = gcp_ref[last:last + 1, :].astype(_F32) * hbp_ref[last:last + 1, :].astype(_F32)
    u_next = gcn_ref[0:1, :].astype(_F32) * hbn_ref[0:1, :].astype(_F32)
    u_prev = jnp.where(pos == 0, 0.0, u_prev)
    u_next = jnp.where(pos == tiles_per_seq - 1, 0.0, u_next)
    row = lax.broadcasted_iota(jnp.int32, (ts, 1), 0)
    up = jnp.where(row == 0, u_prev, pltpu.roll(u, 1, 0))
    un = jnp.where(row == ts - 1, u_next, pltpu.roll(u, ts - 1, 0))
    y = gb_ref[...].astype(_F32) * (w_ref[0:1, :] * up + w_ref[1:2, :] * u + w_ref[2:3, :] * un)
    o_ref[...] = (y * _row_rms_inv(y) * g_ref[...]).astype(o_ref.dtype)


def _mixer_b(proj_r, conv_w, gain, seq):
    m = proj_r.shape[0]
    wb = gain.shape[0]
    ts = _pick_tile(seq, (512, 256, 128))
    halo = BF16_SUBLANE_TILE
    per = ts // halo
    nhalo = m // halo
    main = lambda c: pl.BlockSpec((ts, wb), lambda i: (i, c))
    prev = lambda c: pl.BlockSpec((halo, wb), lambda i: (jnp.maximum(i * per - 1, 0), c))
    nxt = lambda c: pl.BlockSpec((halo, wb), lambda i: (jnp.minimum((i + 1) * per, nhalo - 1), c))
    return pl.pallas_call(
        functools.partial(_mixer_b_body, tiles_per_seq=seq // ts),
        grid=(m // ts,),
        in_specs=[main(0), main(1), main(2), prev(1), prev(2), nxt(1), nxt(2),
                  pl.BlockSpec((CONV_WIDTH, wb), lambda i: (0, 0)),
                  pl.BlockSpec((1, wb), lambda i: (0, 0))],
        out_specs=pl.BlockSpec((ts, wb), lambda i: (i, 0)),
        out_shape=jax.ShapeDtypeStruct((m, wb), _BF16),
        compiler_params=_params(("parallel",)),
        name="mixer_b",
    )(proj_r, proj_r, proj_r, proj_r, proj_r, proj_r, proj_r, conv_w, gain.reshape(1, wb))


def _rope_tables(seq):
    quarter = HEAD_DIM // 4
    inv = ROPE_THETA ** (-jnp.arange(quarter, dtype=_F32) / quarter)
    t = jnp.arange(seq)
    ang_r = (t // GRID_W).astype(_F32)[:, None] * inv[None, :]
    ang_c = (t % GRID_W).astype(_F32)[:, None] * inv[None, :]
    zero = jnp.zeros_like(ang_r)
    cos = jnp.concatenate([jnp.cos(ang_r), jnp.cos(ang_r), jnp.cos(ang_c), jnp.cos(ang_c)], axis=-1)
    sin_hi = jnp.concatenate([-jnp.sin(ang_r), zero, -jnp.sin(ang_c), zero], axis=-1)
    sin_lo = jnp.concatenate([zero, jnp.sin(ang_r), zero, jnp.sin(ang_c)], axis=-1)
    return cos, sin_hi, sin_lo


def _attn_c_body(q_ref, k_ref, v_ref, o_ref, vt_ref, qt_ref, acc_ref, *bufs, tk):
    nchunk = vt_ref.shape[0]
    ntile = qt_ref.shape[0]
    tq = q_ref.shape[0] // ntile
    nq = GQA_GROUP * tq

    @pl.when(pl.program_id(2) == 0)
    def _():
        def transpose_v(c, carry):
            rows = pl.ds(pl.multiple_of(c * tk, tk), tk)
            vt_ref[c, 0:HEAD_DIM, :] = v_ref[rows, :].astype(_F32).T.astype(_BF16)
            vt_ref[c, HEAD_DIM:, :] = jnp.ones((BF16_SUBLANE_TILE, tk), _BF16)
            return carry

        lax.fori_loop(0, nchunk, transpose_v, 0)

    for t in range(ntile):
        for g in range(GQA_GROUP):
            q = q_ref[t * tq:(t + 1) * tq, g * HEAD_DIM:(g + 1) * HEAD_DIM]
            qt_ref[t, :, g * tq:(g + 1) * tq] = q.astype(_F32).T.astype(_BF16)
    acc_ref[...] = jnp.zeros(acc_ref.shape, _F32)

    s_sets = (bufs[0:2], bufs[2:4])
    p_sets = (bufs[4:6], bufs[6:8])
    npair = nchunk // 2
    nitem = ntile * npair

    def stage_scores(item, s_set):
        tile, pair = item // npair, item % npair
        out = []
        for j in range(2):
            rows = pl.ds(pl.multiple_of((2 * pair + j) * tk, tk), tk)
            s = jnp.dot(k_ref[rows, :], qt_ref[tile], preferred_element_type=_F32)
            s_set[j][...] = s
            out.append(jnp.max(s, axis=0, keepdims=True))
        return tuple(out)

    def stage_exp(item, s_set, p_set, maxes, m_run):
        m_run = jnp.where(item % npair == 0, NEG, m_run)
        alphas = []
        for j in range(2):
            m_new = jnp.maximum(m_run, maxes[j])
            alphas.append(jnp.exp2(m_run - m_new))
            p_set[j][...] = jnp.exp2((s_set[j][...] - m_new).astype(_BF16))
            m_run = m_new
        return tuple(alphas), m_run

    def stage_pv(item, p_set, alphas):
        tile, pair = item // npair, item % npair
        for j in range(2):
            acc_ref[tile] = alphas[j] * acc_ref[tile] + jnp.dot(vt_ref[2 * pair + j], p_set[j][...],
                                                                 preferred_element_type=_F32)

    m_run = jnp.full((1, nq), NEG, _F32)
    mx0 = stage_scores(0, s_sets[0])
    mx1 = stage_scores(1, s_sets[1])
    al0, m_run = stage_exp(0, s_sets[0], p_sets[0], mx0, m_run)

    def two_steps(it, carry):
        m_run, mx1, al0 = carry
        j = 2 * it
        mx0 = stage_scores(j + 2, s_sets[0])
        al1, m_run = stage_exp(j + 1, s_sets[1], p_sets[1], mx1, m_run)
        stage_pv(j, p_sets[0], al0)
        mx1 = stage_scores(j + 3, s_sets[1])
        al0, m_run = stage_exp(j + 2, s_sets[0], p_sets[0], mx0, m_run)
        stage_pv(j + 1, p_sets[1], al1)
        return m_run, mx1, al0

    m_run, mx1, al0 = lax.fori_loop(0, nitem // 2 - 1, two_steps, (m_run, mx1, al0))
    al1, m_run = stage_exp(nitem - 1, s_sets[1], p_sets[1], mx1, m_run)
    stage_pv(nitem - 2, p_sets[0], al0)
    stage_pv(nitem - 1, p_sets[1], al1)
    for t in range(ntile):
        out_t = acc_ref[t, 0:HEAD_DIM, :] / acc_ref[t, HEAD_DIM:HEAD_DIM + 1, :]
        for g in range(GQA_GROUP):
            o_ref[t * tq:(t + 1) * tq, g * HEAD_DIM:(g + 1) * HEAD_DIM] = (
                out_t[:, g * tq:(g + 1) * tq].T.astype(o_ref.dtype))


def _attn_c(proj_r, batch, seq, q_heads, kv_heads, q_col0):
    m = proj_r.shape[0]
    assert q_heads == GQA_GROUP * kv_heads and q_col0 % (GQA_GROUP * HEAD_DIM) == 0
    q_blk0 = q_col0 // (GQA_GROUP * HEAD_DIM)
    k_blk0 = q_col0 // HEAD_DIM + q_heads
    v_blk0 = k_blk0 + kv_heads
    tq = _pick_tile(seq, (256, 128))
    ntile = next(t for t in (4, 2, 1) if seq % (t * tq) == 0)
    tk = _pick_tile(seq, (512, 256, 128))
    assert (seq // tk) % 4 == 0, "the chunk pipeline advances two pairs of key chunks per loop step"
    tq_step = ntile * tq
    nq = seq // tq_step
    gw = GQA_GROUP * HEAD_DIM
    return pl.pallas_call(
        functools.partial(_attn_c_body, tk=tk),
        grid=(batch, kv_heads, nq),
        in_specs=[pl.BlockSpec((tq_step, gw), lambda b, g, i: (b * nq + i, q_blk0 + g)),
                  pl.BlockSpec((seq, HEAD_DIM), lambda b, g, i: (b, k_blk0 + g)),
                  pl.BlockSpec((seq, HEAD_DIM), lambda b, g, i: (b, v_blk0 + g))],
        out_specs=pl.BlockSpec((tq_step, gw), lambda b, g, i: (b * nq + i, g)),
        out_shape=jax.ShapeDtypeStruct((m, q_heads * HEAD_DIM), _BF16),
        scratch_shapes=[pltpu.VMEM((seq // tk, HEAD_DIM + BF16_SUBLANE_TILE, tk), _BF16),
                        pltpu.VMEM((ntile, HEAD_DIM, GQA_GROUP * tq), _BF16),
                        pltpu.VMEM((ntile, HEAD_DIM + BF16_SUBLANE_TILE, GQA_GROUP * tq), _F32)]
        + [pltpu.VMEM((tk, GQA_GROUP * tq), _F32)] * 4
        + [pltpu.VMEM((tk, GQA_GROUP * tq), _BF16)] * 4,
        compiler_params=_params(("parallel", "parallel", "arbitrary")),
        name="attn_c",
    )(proj_r, proj_r, proj_r)


def _out_proj_body(oa_ref, ob_ref, oc_ref, ga_ref, gc_ref, w_ref, x_ref, gp_ref, o_ref, xn_ref, *, tk):
    k = pl.program_id(1)
    nk = pl.num_programs(1)

    @pl.when(k == 0)
    def _():
        def fn(rows):
            c0 = 0
            for ref, g_ref in ((oa_ref, ga_ref), (ob_ref, None), (oc_ref, gc_ref)):
                inv = None if g_ref is None else _rows_rms_inv(ref, rows)
                for cols in _lane_tiles(ref.shape[1]):
                    val = ref[rows, cols]
                    if g_ref is not None:
                        val = (val.astype(_F32) * inv * g_ref[:, cols]).astype(_BF16)
                    xn_ref[c0 + cols.start // tk, rows, cols.start % tk:cols.start % tk + LANES] = val
                c0 += ref.shape[1] // tk

        _for_row_chunks(oa_ref.shape[0], fn)
        o_ref[...] = jnp.dot(xn_ref[0], w_ref[...], preferred_element_type=_F32)

    @pl.when(k > 0)
    def _():
        o_ref[...] += jnp.dot(xn_ref[k], w_ref[...], preferred_element_type=_F32)

    @pl.when(k == nk - 1)
    def _():
        _residual_norm_rows(o_ref, x_ref, gp_ref)


def _out_proj(oa, ob, oc, ga, gc, w, layer, x, gp):
    m, d = x.shape
    kdim = w.shape[1]
    wa, wb, wc = oa.shape[1], ob.shape[1], oc.shape[1]
    tm = _pick_tile(m, (512, 256, 128))
    tk = next(t for t in (512, 256, 128) if wa % t == 0 and wb % t == 0 and wc % t == 0)
    nk = kdim // tk
    row = lambda width: pl.BlockSpec((tm, width), lambda i, k: (i, 0))
    vec = lambda width: pl.BlockSpec((1, width), lambda i, k: (0, 0))
    return pl.pallas_call(
        functools.partial(_out_proj_body, tk=tk),
        grid=(m // tm, nk),
        in_specs=[row(wa), row(wb), row(wc), vec(wa), vec(wc),
                  pl.BlockSpec((None, tk, d), lambda i, k: (layer, k, 0)),
                  row(d), vec(d)],
        out_specs=row(d),
        out_shape=jax.ShapeDtypeStruct((m, d), _F32),
        scratch_shapes=[pltpu.VMEM((nk, tm, tk), _BF16)],
        compiler_params=_params(("parallel", "arbitrary")),
        name="out_proj",
    )(oa, ob, oc, ga.reshape(1, wa), gc.reshape(1, wc), w, x, gp.reshape(1, d))


def _mlp_body(x_ref, g1_ref, wu_ref, wd_ref, g2_ref, o_ref, xn_ref, hid0_ref, hid1_ref, *, nf):
    f = pl.program_id(1)
    hid_refs = (hid0_ref, hid1_ref)

    def up(dst_ref):
        hid = jnp.dot(xn_ref[...], wu_ref[...], preferred_element_type=_F32)
        dst_ref[...] = jnp.square(jnp.maximum(hid, 0.0)).astype(_BF16)

    def down(src_ref):
        o_ref[...] += jnp.dot(src_ref[...], wd_ref[...], preferred_element_type=_F32)

    @pl.when(f == 0)
    def _():
        _norm_rows_to(xn_ref, x_ref, g1_ref)
        o_ref[...] = jnp.zeros(o_ref.shape, _F32)
        up(hid_refs[0])

    for parity in range(2):
        @pl.when((f > 0) & (f < nf) & (f % 2 == parity))
        def _(parity=parity):
            up(hid_refs[parity])
            down(hid_refs[1 - parity])

    @pl.when(f == nf)
    def _():
        down(hid_refs[(nf - 1) % 2])
        _residual_norm_rows(o_ref, x_ref, g2_ref)


def _mlp(x, g1, wu, wd, layer, g2):
    m, d = x.shape
    dff = wu.shape[2]
    tm = _pick_tile(m, (512, 256, 128))
    tf = _pick_tile(dff, (512, 256, 128))
    row = pl.BlockSpec((tm, d), lambda i, f: (i, 0))
    vec = pl.BlockSpec((1, d), lambda i, f: (0, 0))
    nf = dff // tf
    return pl.pallas_call(
        functools.partial(_mlp_body, nf=nf),
        grid=(m // tm, nf + 1),
        in_specs=[row, vec,
                  pl.BlockSpec((None, d, tf), lambda i, f: (layer, 0, jnp.minimum(f, nf - 1))),
                  pl.BlockSpec((None, tf, d), lambda i, f: (layer, jnp.maximum(f - 1, 0), 0)),
                  vec],
        out_specs=row,
        out_shape=jax.ShapeDtypeStruct((m, d), _F32),
        scratch_shapes=[pltpu.VMEM((tm, d), _BF16), pltpu.VMEM((tm, tf), _BF16), pltpu.VMEM((tm, tf), _BF16)],
        compiler_params=_params(("parallel", "arbitrary")),
        name="mlp",
    )(x, g1.reshape(1, d), wu, wd, g2.reshape(1, d))


def kernel(x, rel_bias, pre_mix_norm, w_in, conv_w, q_norm, k_norm, out_norm_a, out_norm_b, out_norm_c,
           w_out, post_mix_norm, pre_mlp_norm, w_up, w_down, post_mlp_norm):
    batch, seq, d = x.shape
    depth = w_in.shape[0]
    wa, wb, wc = out_norm_a.shape[1], out_norm_b.shape[1], out_norm_c.shape[1]
    in_width = w_in.shape[2]
    kv_width = (in_width - 3 * wa - 3 * wb - wc) // 2
    a_heads, q_heads, kv_heads = wa // HEAD_DIM, wc // HEAD_DIM, kv_width // HEAD_DIM
    assert all(w // (2 * dil) == SPAN for w, dil in A_BRANCHES)
    assert rel_bias.shape == (NUM_BUCKETS, a_heads) and seq % GRID_W == 0

    idx_tbl = jnp.asarray(_bucket_table())
    tables = _rope_tables(seq)
    xf = x.reshape(batch * seq, d)
    w_in, w_out, w_up, w_down = (_cast_bf16(w) for w in (w_in, w_out, w_up, w_down))
    for i in range(depth):
        q0 = 3 * wb
        proj_a, proj_r = _in_proj(xf, pre_mix_norm[i], w_in, i, 3 * wa, jnp.stack([q_norm[i], k_norm[i]]), tables, seq,
                                  (q0, q0 + wc), (q0 + wc, q0 + wc + kv_width))
        oa = _mixer_a(proj_a, rel_bias, idx_tbl, batch, seq, a_heads)
        ob = _mixer_b(proj_r, conv_w[i], out_norm_b[i], seq)
        oc = _attn_c(proj_r, batch, seq, q_heads, kv_heads, q0)
        xf = _out_proj(oa, ob, oc, out_norm_a[i], out_norm_c[i], w_out, i, xf, post_mix_norm[i])
        xf = _mlp(xf, pre_mlp_norm[i], w_up, w_down, i, post_mlp_norm[i])
    return xf.reshape(batch, seq, d)
```

```python
import functools
import math

import numpy as np
import jax
import jax.numpy as jnp
from jax import lax
from jax.experimental import pallas as pl
from jax.experimental.pallas import tpu as pltpu

HEAD_DIM = 128
A_BRANCHES = ((128, 1), (512, 4), (2048, 16))
SPAN = 64
Q_TILE_A = 2 * SPAN
K_TILE_A = 4 * SPAN
GROUP_A = 8
ROPE_THETA = 10000.0
GRID_W = 64
NUM_BUCKETS = 32
MAX_DISTANCE = 1024
CONV_WIDTH = 3
GQA_GROUP = 4
EPS = 1e-6
NEG = -1e30
Q_SCALE = HEAD_DIM ** -0.5
LOG2_E = math.log2(math.e)

V7X_VMEM_LIMIT_BYTES = 60 * 1024 * 1024
BF16_SUBLANE_TILE = 16

_F32 = jnp.float32
_BF16 = jnp.bfloat16


def _pick_tile(n, prefs):
    for t in prefs:
        if n % t == 0:
            return t
    return n


def _params(sem):
    return pltpu.CompilerParams(dimension_semantics=sem, vmem_limit_bytes=V7X_VMEM_LIMIT_BYTES)


def _row_rms_inv(v):
    return lax.rsqrt(jnp.mean(v * v, axis=-1, keepdims=True) + EPS)


ROW_CHUNK = 64
LANES = 128


def _for_row_chunks(nrows, fn):
    def step(c, carry):
        fn(pl.ds(pl.multiple_of(c * ROW_CHUNK, ROW_CHUNK), ROW_CHUNK))
        return carry

    lax.fori_loop(0, nrows // ROW_CHUNK, step, 0)


def _lane_tiles(width):
    return [slice(c, c + LANES) for c in range(0, width, LANES)]


def _rows_rms_inv(ref, rows):
    width = ref.shape[1]
    acc = None
    for cols in _lane_tiles(width):
        blk = ref[rows, cols].astype(_F32)
        acc = blk * blk if acc is None else acc + blk * blk
    inv = lax.rsqrt(jnp.sum(acc, axis=-1, keepdims=True) / width + EPS)
    return jnp.broadcast_to(inv, acc.shape)


def _norm_rows_to(dst_ref, src_ref, g_ref):
    def fn(rows):
        inv = _rows_rms_inv(src_ref, rows)
        for cols in _lane_tiles(src_ref.shape[1]):
            dst_ref[rows, cols] = (src_ref[rows, cols].astype(_F32) * inv * g_ref[:, cols]).astype(dst_ref.dtype)

    _for_row_chunks(src_ref.shape[0], fn)


def _residual_norm_rows(o_ref, x_ref, g_ref):
    def fn(rows):
        inv = _rows_rms_inv(o_ref, rows)
        for cols in _lane_tiles(o_ref.shape[1]):
            o_ref[rows, cols] = x_ref[rows, cols] + o_ref[rows, cols] * inv * g_ref[:, cols]

    _for_row_chunks(o_ref.shape[0], fn)


CAST_BLOCK_BYTES = 8 * 1024 * 1024


def _cast_body(w_ref, o_ref):
    o_ref[...] = w_ref[...].astype(o_ref.dtype)


def _cast_bf16(w, layer):
    _, k, n = w.shape
    tr = _pick_tile(k, [t for t in (2048, 1024, 512, 256, 128, 64, 32, 16) if t * n * 4 <= CAST_BLOCK_BYTES])
    return pl.pallas_call(
        _cast_body,
        grid=(k // tr,),
        in_specs=[pl.BlockSpec((None, tr, n), lambda r: (layer, r, 0))],
        out_specs=pl.BlockSpec((tr, n), lambda r: (r, 0)),
        out_shape=jax.ShapeDtypeStruct((k, n), _BF16),
        compiler_params=_params(("parallel",)),
        name="cast_bf16",
    )(w)


def _rope_head(v, gain, cos, sin_hi, sin_lo, scale):
    quarter = HEAD_DIM // 4
    lanes = v.ndim - 1
    xn = v * (_row_rms_inv(v) * scale) * gain
    return xn * cos + pltpu.roll(xn, HEAD_DIM - quarter, lanes) * sin_hi + pltpu.roll(xn, quarter, lanes) * sin_lo


def _in_proj_body(x_ref, g_ref, w_ref, hg_ref, cos_ref, shi_ref, slo_ref, oa_ref, or_ref, xn_ref, *, n_a, kinds):
    j = pl.program_id(1)

    @pl.when(j == 0)
    def _():
        _norm_rows_to(xn_ref, x_ref, g_ref)

    @pl.when(j < n_a)
    def _():
        oa_ref[...] = jnp.dot(xn_ref[...], w_ref[...], preferred_element_type=_F32)

    for sig in sorted(set(kinds)):
        tiles = [t for t, k in enumerate(kinds) if k == sig]

        @pl.when(functools.reduce(jnp.logical_or, [j == n_a + t for t in tiles]))
        def _(sig=sig):
            if all(kind == "p" for kind in sig):
                or_ref[...] = jnp.dot(xn_ref[...], w_ref[...], preferred_element_type=_F32).astype(_BF16)
                return
            res = jnp.dot(xn_ref[...], w_ref[...], preferred_element_type=_F32)
            head = lambda s: res[:, s * HEAD_DIM:(s + 1) * HEAD_DIM]
            for kind, gain_row, scale in (("q", 0, Q_SCALE * LOG2_E), ("k", 1, 1.0)):
                slots = [s for s, k in enumerate(sig) if k == kind]
                if slots:
                    v = jnp.stack([head(s) for s in slots])
                    v = _rope_head(v, hg_ref[gain_row:gain_row + 1, :], cos_ref[...], shi_ref[...], slo_ref[...], scale)
                    for n, s in enumerate(slots):
                        or_ref[:, s * HEAD_DIM:(s + 1) * HEAD_DIM] = v[n].astype(_BF16)
            for s, k in enumerate(sig):
                if k == "p":
                    or_ref[:, s * HEAD_DIM:(s + 1) * HEAD_DIM] = head(s).astype(_BF16)


def _in_proj(x, g, w, a_cols, head_gains, tables, seq, q_cols, k_cols):
    m, d = x.shape
    n = w.shape[1]
    tm = _pick_tile(seq, (512, 256, 128))
    tn = next(t for t in (1024, 512, 256, 128) if a_cols % t == 0 and (n - a_cols) % t == 0)
    n_a = a_cols // tn

    def kind(col):
        return "q" if q_cols[0] <= col < q_cols[1] else "k" if k_cols[0] <= col < k_cols[1] else "p"

    kinds = tuple(tuple(kind(t * tn + s * HEAD_DIM) for s in range(tn // HEAD_DIM)) for t in range((n - a_cols) // tn))
    per_seq = seq // tm
    tab = pl.BlockSpec((tm, HEAD_DIM), lambda i, j: (i % per_seq, 0))
    return pl.pallas_call(
        functools.partial(_in_proj_body, n_a=n_a, kinds=kinds),
        grid=(m // tm, n // tn),
        in_specs=[
            pl.BlockSpec((tm, d), lambda i, j: (i, 0)),
            pl.BlockSpec((1, d), lambda i, j: (0, 0)),
            pl.BlockSpec((d, tn), lambda i, j: (0, j)),
            pl.BlockSpec((2, HEAD_DIM), lambda i, j: (0, 0)),
            tab, tab, tab,
        ],
        out_specs=[
            pl.BlockSpec((tm, tn), lambda i, j: (i, jnp.minimum(j, n_a - 1))),
            pl.BlockSpec((tm, tn), lambda i, j: (i, jnp.maximum(j - n_a, 0))),
        ],
        out_shape=[
            jax.ShapeDtypeStruct((m, a_cols), _F32),
            jax.ShapeDtypeStruct((m, n - a_cols), _BF16),
        ],
        scratch_shapes=[pltpu.VMEM((tm, d), _BF16)],
        compiler_params=_params(("parallel", "arbitrary")),
        name="in_proj",
    )(x, g.reshape(1, d), w, head_gains, *tables)


def _bucket_table():
    half = NUM_BUCKETS // 2
    max_exact = half // 2
    i = np.arange(Q_TILE_A)[:, None]
    c = np.arange(K_TILE_A)[None, :]
    out = np.zeros((len(A_BRANCHES), 3, Q_TILE_A, K_TILE_A), np.int32)
    for br, (_, dil) in enumerate(A_BRANCHES):
        for var in range(3):
            rel = c - SPAN * var - i
            dist = rel * dil
            n = np.abs(dist)
            t = np.log(np.maximum(n, 1) / max_exact) / math.log(MAX_DISTANCE / max_exact) * (half - max_exact)
            valid = np.abs(rel) <= SPAN
            frac = np.abs(t - np.round(t))
            assert np.all((frac > 1e-4) | (n <= max_exact) | (n >= MAX_DISTANCE) | ~valid), "bucket edge near an integer"
            large = np.minimum(max_exact + t.astype(np.int32), half - 1)
            bucket = np.where(dist > 0, half, 0) + np.where(n < max_exact, n, large)
            out[br, var] = np.where(valid, bucket, NUM_BUCKETS)
    return out


def _mixer_a_body(rel_ref, idx_ref, q_ref, k_ref, v_ref, o_ref, bias_ref, acc_ref, m_ref, l_ref, *, seq):
    h = pl.program_id(1)

    for br in range(len(A_BRANCHES)):
        for var in range(3):
            idx = idx_ref[br, var]

            def fill(bkt, bias, idx=idx):
                return jnp.where(idx == bkt, rel_ref[bkt, h], bias)

            bias_ref[br, var] = lax.fori_loop(0, NUM_BUCKETS, fill, jnp.full(idx.shape, NEG, _F32))

    order = sorted(range(len(A_BRANCHES)), key=lambda b: -A_BRANCHES[b][1])
    for br in order:
        dil = A_BRANCHES[br][1]
        length = seq // dil
        nblk = length // Q_TILE_A
        first = br == order[0]

        def rows(start, size, dil=dil):
            if dil == 1:
                return pl.ds(pl.multiple_of(start, SPAN), size)
            return pl.ds(start, size, stride=dil)

        def group(blocks, br=br, dil=dil, length=length, nblk=nblk, first=first, rows=rows):
            qsls, ss, vs = [], [], []
            whole = {}
            for n, r in blocks:
                p0 = n * Q_TILE_A
                qsl = rows(r + dil * p0, Q_TILE_A)
                q = (q_ref[qsl, :] * Q_SCALE).astype(_BF16)
                if isinstance(n, int):
                    ks = min(max(p0 - SPAN, 0), length - K_TILE_A)
                    var = 0 if n == 0 else 2 if n == nblk - 1 else 1
                    if id(r) not in whole:
                        seq_rows = rows(r, length)
                        whole[id(r)] = (k_ref[seq_rows, :].astype(_BF16), v_ref[seq_rows, :].astype(_BF16))
                    k, v = (t[ks:ks + K_TILE_A] for t in whole[id(r)])
                else:
                    ks = jnp.clip(p0 - SPAN, 0, length - K_TILE_A)
                    var = jnp.where(n == 0, 0, jnp.where(n == nblk - 1, 2, 1))
                    ksl = rows(r + dil * ks, K_TILE_A)
                    k, v = k_ref[ksl, :].astype(_BF16), v_ref[ksl, :].astype(_BF16)
                s = lax.dot_general(q, k, (((1,), (1,)), ((), ())), preferred_element_type=_F32)
                qsls.append(qsl)
                ss.append(s + bias_ref[br, var])
                vs.append(v)
            s = jnp.stack(ss)
            m_blk = jnp.max(s, axis=-1, keepdims=True)
            if first:
                m_new = m_blk
                p = jnp.exp(s - m_new)
            else:
                m_old = jnp.stack([m_ref[qsl, :] for qsl in qsls])
                m_new = jnp.maximum(m_old, m_blk)
                p = jnp.exp(s - jnp.concatenate([m_new] * (K_TILE_A // HEAD_DIM), axis=-1))
            l_blk = jnp.sum(p, axis=-1, keepdims=True)
            p = p.astype(_BF16)
            pv = jnp.stack([jnp.dot(p[j], vs[j], preferred_element_type=_F32) for j in range(GROUP_A)])
            if first:
                acc_new = pv
                l_new = jnp.broadcast_to(l_blk, pv.shape)
                m_new = jnp.broadcast_to(m_new, pv.shape)
            else:
                alpha = jnp.exp(m_old - m_new)
                acc_new = alpha * jnp.stack([acc_ref[qsl, :] for qsl in qsls]) + pv
                l_new = alpha * jnp.stack([l_ref[qsl, :] for qsl in qsls]) + l_blk
            for j, qsl in enumerate(qsls):
                acc_ref[qsl, :] = acc_new[j]
                l_ref[qsl, :] = l_new[j]
                m_ref[qsl, :] = m_new[j]

        if nblk <= GROUP_A:
            res_per_group = GROUP_A // nblk

            def body(rg, carry, group=group, res_per_group=res_per_group, nblk=nblk):
                residues = [rg * res_per_group + jr for jr in range(res_per_group)]
                group([(n, r) for r in residues for n in range(nblk)])
                return carry

            lax.fori_loop(0, dil // res_per_group, body, 0)
        else:
            res_per_group = min(dil, GROUP_A)
            blk_per_group = GROUP_A // res_per_group
            ngrp_n = nblk // blk_per_group

            def body(it, carry, group=group, res_per_group=res_per_group, blk_per_group=blk_per_group, ngrp_n=ngrp_n):
                rg, ng = it // ngrp_n, it % ngrp_n
                group([(ng * blk_per_group + jn, rg * res_per_group + jr)
                       for jn in range(blk_per_group) for jr in range(res_per_group)])
                return carry

            lax.fori_loop(0, (dil // res_per_group) * ngrp_n, body, 0)

    chunk = _pick_tile(seq, (512, 256, 128))

    def finish(c, carry):
        sl = pl.ds(pl.multiple_of(c * chunk, chunk), chunk)
        o_ref[sl, :] = (acc_ref[sl, :] / l_ref[sl, :]).astype(o_ref.dtype)
        return carry

    lax.fori_loop(0, seq // chunk, finish, 0)


def _mixer_a(proj_a, rel_bias, idx_tbl, batch, seq, heads):
    m = proj_a.shape[0]
    assert seq % (A_BRANCHES[-1][1] * K_TILE_A) == 0, "sequence too short for the widest dilation"
    for _, dil in A_BRANCHES:
        nblk = seq // (dil * Q_TILE_A)
        if nblk <= GROUP_A:
            assert GROUP_A % nblk == 0 and dil % (GROUP_A // nblk) == 0
        else:
            assert GROUP_A % min(dil, GROUP_A) == 0 and nblk % (GROUP_A // min(dil, GROUP_A)) == 0
    blk = (seq, HEAD_DIM)
    return pl.pallas_call(
        functools.partial(_mixer_a_body, seq=seq),
        grid=(batch, heads),
        in_specs=[
            pl.BlockSpec(memory_space=pltpu.SMEM),
            pl.BlockSpec(idx_tbl.shape, lambda b, h: (0, 0, 0, 0)),
            pl.BlockSpec(blk, lambda b, h: (b, h)),
            pl.BlockSpec(blk, lambda b, h: (b, heads + h)),
            pl.BlockSpec(blk, lambda b, h: (b, 2 * heads + h)),
        ],
        out_specs=pl.BlockSpec(blk, lambda b, h: (b, h)),
        out_shape=jax.ShapeDtypeStruct((m, heads * HEAD_DIM), _BF16),
        scratch_shapes=[
            pltpu.VMEM(idx_tbl.shape, _F32),
            pltpu.VMEM(blk, _F32),
            pltpu.VMEM(blk, _F32),
            pltpu.VMEM(blk, _F32),
        ],
        compiler_params=_params(("parallel", "parallel")),
        name="mixer_a",
    )(rel_bias, idx_tbl, proj_a, proj_a, proj_a)


def _mixer_b_body(gb_ref, gc_ref, hb_ref, gcp_ref, hbp_ref, gcn_ref, hbn_ref, w_ref, g_ref, o_ref, *, tiles_per_seq):
    i = pl.program_id(0)
    ts = gb_ref.shape[0]
    pos = i % tiles_per_seq
    u = gc_ref[...].astype(_F32) * hb_ref[...].astype(_F32)
    last = BF16_SUBLANE_TILE - 1
    u_prev = gcp_ref[last:last + 1, :].astype(_F32) * hbp_ref[last:last + 1, :].astype(_F32)
    u_next = gcn_ref[0:1, :].astype(_F32) * hbn_ref[0:1, :].astype(_F32)
    u_prev = jnp.where(pos == 0, 0.0, u_prev)
    u_next = jnp.where(pos == tiles_per_seq - 1, 0.0, u_next)
    row = lax.broadcasted_iota(jnp.int32, (ts, 1), 0)
    up = jnp.where(row == 0, u_prev, pltpu.roll(u, 1, 0))
    un = jnp.where(row == ts - 1, u_next, pltpu.roll(u, ts - 1, 0))
    y = gb_ref[...].astype(_F32) * (w_ref[0:1, :] * up + w_ref[1:2, :] * u + w_ref[2:3, :] * un)
    o_ref[...] = (y * _row_rms_inv(y) * g_ref[...]).astype(o_ref.dtype)


def _mixer_b(proj_r, conv_w, gain, seq):
    m = proj_r.shape[0]
    wb = gain.shape[0]
    ts = _pick_tile(seq, (512, 256, 128))
    halo = BF16_SUBLANE_TILE
    per = ts // halo
    nhalo = m // halo
    main = lambda c: pl.BlockSpec((ts, wb), lambda i: (i, c))
    prev = lambda c: pl.BlockSpec((halo, wb), lambda i: (jnp.maximum(i * per - 1, 0), c))
    nxt = lambda c: pl.BlockSpec((halo, wb), lambda i: (jnp.minimum((i + 1) * per, nhalo - 1), c))
    return pl.pallas_call(
        functools.partial(_mixer_b_body, tiles_per_seq=seq // ts),
        grid=(m // ts,),
        in_specs=[main(0), main(1), main(2), prev(1), prev(2), nxt(1), nxt(2),
                  pl.BlockSpec((CONV_WIDTH, wb), lambda i: (0, 0)),
                  pl.BlockSpec((1, wb), lambda i: (0, 0))],
        out_specs=pl.BlockSpec((ts, wb), lambda i: (i, 0)),
        out_shape=jax.ShapeDtypeStruct((m, wb), _BF16),
        compiler_params=_params(("parallel",)),
        name="mixer_b",
    )(proj_r, proj_r, proj_r, proj_r, proj_r, proj_r, proj_r, conv_w, gain.reshape(1, wb))


def _rope_tables(seq):
    quarter = HEAD_DIM // 4
    inv = ROPE_THETA ** (-jnp.arange(quarter, dtype=_F32) / quarter)
    t = jnp.arange(seq)
    ang_r = (t // GRID_W).astype(_F32)[:, None] * inv[None, :]
    ang_c = (t % GRID_W).astype(_F32)[:, None] * inv[None, :]
    zero = jnp.zeros_like(ang_r)
    cos = jnp.concatenate([jnp.cos(ang_r), jnp.cos(ang_r), jnp.cos(ang_c), jnp.cos(ang_c)], axis=-1)
    sin_hi = jnp.concatenate([-jnp.sin(ang_r), zero, -jnp.sin(ang_c), zero], axis=-1)
    sin_lo = jnp.concatenate([zero, jnp.sin(ang_r), zero, jnp.sin(ang_c)], axis=-1)
    return cos, sin_hi, sin_lo


def _attn_c_body(q_ref, k_ref, v_ref, *rest, tk, ncast):
    w_refs, o_ref, wo_refs = rest[:ncast], rest[ncast], rest[ncast + 1:2 * ncast + 1]
    vt_ref, qt_ref, acc_ref, *bufs = rest[2 * ncast + 1:]
    nchunk = vt_ref.shape[0]
    ntile = qt_ref.shape[0]
    tq = q_ref.shape[0] // ntile
    nq = GQA_GROUP * tq

    @pl.when(pl.program_id(2) == 0)
    def _():
        def transpose_v(c, carry):
            rows = pl.ds(pl.multiple_of(c * tk, tk), tk)
            vt_ref[c, 0:HEAD_DIM, :] = v_ref[rows, :].astype(_F32).T.astype(_BF16)
            vt_ref[c, HEAD_DIM:, :] = jnp.ones((BF16_SUBLANE_TILE, tk), _BF16)
            return carry

        lax.fori_loop(0, nchunk, transpose_v, 0)

    for t in range(ntile):
        for g in range(GQA_GROUP):
            q = q_ref[t * tq:(t + 1) * tq, g * HEAD_DIM:(g + 1) * HEAD_DIM]
            qt_ref[t, :, g * tq:(g + 1) * tq] = q.astype(_F32).T.astype(_BF16)
    acc_ref[...] = jnp.zeros(acc_ref.shape, _F32)
    for w_ref, wo_ref in zip(w_refs, wo_refs):
        wo_ref[...] = w_ref[...].astype(wo_ref.dtype)

    s_sets = (bufs[0:2], bufs[2:4])
    p_sets = (bufs[4:6], bufs[6:8])
    npair = nchunk // 2
    nitem = ntile * npair

    def stage_scores(item, s_set):
        tile, pair = item // npair, item % npair
        out = []
        for j in range(2):
            rows = pl.ds(pl.multiple_of((2 * pair + j) * tk, tk), tk)
            s = jnp.dot(k_ref[rows, :], qt_ref[tile], preferred_element_type=_F32)
            s_set[j][...] = s
            out.append(jnp.max(s, axis=0, keepdims=True))
        return tuple(out)

    def stage_exp(item, s_set, p_set, maxes, m_run):
        m_run = jnp.where(item % npair == 0, NEG, m_run)
        alphas = []
        for j in range(2):
            m_new = jnp.maximum(m_run, maxes[j])
            alphas.append(jnp.exp2(m_run - m_new))
            p_set[j][...] = jnp.exp2((s_set[j][...] - m_new).astype(_BF16))
            m_run = m_new
        return tuple(alphas), m_run

    def stage_pv(item, p_set, alphas):
        tile, pair = item // npair, item % npair
        for j in range(2):
            acc_ref[tile] = alphas[j] * acc_ref[tile] + jnp.dot(vt_ref[2 * pair + j], p_set[j][...],
                                                                 preferred_element_type=_F32)

    m_run = jnp.full((1, nq), NEG, _F32)
    mx0 = stage_scores(0, s_sets[0])
    mx1 = stage_scores(1, s_sets[1])
    al0, m_run = stage_exp(0, s_sets[0], p_sets[0], mx0, m_run)

    def two_steps(it, carry):
        m_run, mx1, al0 = carry
        j = 2 * it
        mx0 = stage_scores(j + 2, s_sets[0])
        al1, m_run = stage_exp(j + 1, s_sets[1], p_sets[1], mx1, m_run)
        stage_pv(j, p_sets[0], al0)
        mx1 = stage_scores(j + 3, s_sets[1])
        al0, m_run = stage_exp(j + 2, s_sets[0], p_sets[0], mx0, m_run)
        stage_pv(j + 1, p_sets[1], al1)
        return m_run, mx1, al0

    m_run, mx1, al0 = lax.fori_loop(0, nitem // 2 - 1, two_steps, (m_run, mx1, al0))
    al1, m_run = stage_exp(nitem - 1, s_sets[1], p_sets[1], mx1, m_run)
    stage_pv(nitem - 2, p_sets[0], al0)
    stage_pv(nitem - 1, p_sets[1], al1)
    for t in range(ntile):
        out_t = acc_ref[t, 0:HEAD_DIM, :] / acc_ref[t, HEAD_DIM:HEAD_DIM + 1, :]
        for g in range(GQA_GROUP):
            o_ref[t * tq:(t + 1) * tq, g * HEAD_DIM:(g + 1) * HEAD_DIM] = (
                out_t[:, g * tq:(g + 1) * tq].T.astype(o_ref.dtype))


def _attn_c(proj_r, batch, seq, q_heads, kv_heads, q_col0, casts):
    m = proj_r.shape[0]
    assert q_heads == GQA_GROUP * kv_heads and q_col0 % (GQA_GROUP * HEAD_DIM) == 0
    q_blk0 = q_col0 // (GQA_GROUP * HEAD_DIM)
    k_blk0 = q_col0 // HEAD_DIM + q_heads
    v_blk0 = k_blk0 + kv_heads
    tq = _pick_tile(seq, (256, 128))
    ntile = next(t for t in (2, 1) if seq % (t * tq) == 0)
    tk = _pick_tile(seq, (512, 256, 128))
    assert (seq // tk) % 4 == 0, "the chunk pipeline advances two pairs of key chunks per loop step"
    tq_step = ntile * tq
    nq = seq // tq_step
    gw = GQA_GROUP * HEAD_DIM
    nsteps = batch * kv_heads * nq
    step = lambda b, g, i: (b * kv_heads + g) * nq + i
    cast_in, cast_out, cast_shapes = [], [], []
    for w, layer in casts:
        _, k, n = w.shape
        assert k % (nsteps * BF16_SUBLANE_TILE) == 0
        rows = k // nsteps
        cast_in.append(pl.BlockSpec((None, rows, n), lambda b, g, i, layer=layer: (layer, step(b, g, i), 0)))
        cast_out.append(pl.BlockSpec((rows, n), lambda b, g, i: (step(b, g, i), 0)))
        cast_shapes.append(jax.ShapeDtypeStruct((k, n), _BF16))
    outs = pl.pallas_call(
        functools.partial(_attn_c_body, tk=tk, ncast=len(casts)),
        grid=(batch, kv_heads, nq),
        in_specs=[pl.BlockSpec((tq_step, gw), lambda b, g, i: (b * nq + i, q_blk0 + g)),
                  pl.BlockSpec((seq, HEAD_DIM), lambda b, g, i: (b, k_blk0 + g)),
                  pl.BlockSpec((seq, HEAD_DIM), lambda b, g, i: (b, v_blk0 + g))] + cast_in,
        out_specs=[pl.BlockSpec((tq_step, gw), lambda b, g, i: (b * nq + i, g))] + cast_out,
        out_shape=[jax.ShapeDtypeStruct((m, q_heads * HEAD_DIM), _BF16)] + cast_shapes,
        scratch_shapes=[pltpu.VMEM((seq // tk, HEAD_DIM + BF16_SUBLANE_TILE, tk), _BF16),
                        pltpu.VMEM((ntile, HEAD_DIM, GQA_GROUP * tq), _BF16),
                        pltpu.VMEM((ntile, HEAD_DIM + BF16_SUBLANE_TILE, GQA_GROUP * tq), _F32)]
        + [pltpu.VMEM((tk, GQA_GROUP * tq), _F32)] * 4
        + [pltpu.VMEM((tk, GQA_GROUP * tq), _BF16)] * 4,
        compiler_params=_params(("arbitrary", "arbitrary", "arbitrary")),
        name="attn_c",
    )(proj_r, proj_r, proj_r, *[w for w, _ in casts])
    return outs[0], outs[1:]


def _out_proj_body(oa_ref, ob_ref, oc_ref, ga_ref, gc_ref, w_ref, x_ref, gp_ref, o_ref, xn_ref, *, tk):
    k = pl.program_id(1)
    nk = pl.num_programs(1)

    @pl.when(k == 0)
    def _():
        def fn(rows):
            c0 = 0
            for ref, g_ref in ((oa_ref, ga_ref), (ob_ref, None), (oc_ref, gc_ref)):
                inv = None if g_ref is None else _rows_rms_inv(ref, rows)
                for cols in _lane_tiles(ref.shape[1]):
                    val = ref[rows, cols]
                    if g_ref is not None:
                        val = (val.astype(_F32) * inv * g_ref[:, cols]).astype(_BF16)
                    xn_ref[c0 + cols.start // tk, rows, cols.start % tk:cols.start % tk + LANES] = val
                c0 += ref.shape[1] // tk

        _for_row_chunks(oa_ref.shape[0], fn)
        o_ref[...] = jnp.dot(xn_ref[0], w_ref[...], preferred_element_type=_F32)

    @pl.when(k > 0)
    def _():
        o_ref[...] += jnp.dot(xn_ref[k], w_ref[...], preferred_element_type=_F32)

    @pl.when(k == nk - 1)
    def _():
        _residual_norm_rows(o_ref, x_ref, gp_ref)


def _out_proj(oa, ob, oc, ga, gc, w, x, gp):
    m, d = x.shape
    kdim = w.shape[0]
    wa, wb, wc = oa.shape[1], ob.shape[1], oc.shape[1]
    tm = _pick_tile(m, (512, 256, 128))
    tk = next(t for t in (512, 256, 128) if wa % t == 0 and wb % t == 0 and wc % t == 0)
    nk = kdim // tk
    row = lambda width: pl.BlockSpec((tm, width), lambda i, k: (i, 0))
    vec = lambda width: pl.BlockSpec((1, width), lambda i, k: (0, 0))
    return pl.pallas_call(
        functools.partial(_out_proj_body, tk=tk),
        grid=(m // tm, nk),
        in_specs=[row(wa), row(wb), row(wc), vec(wa), vec(wc),
                  pl.BlockSpec((tk, d), lambda i, k: (k, 0)),
                  row(d), vec(d)],
        out_specs=row(d),
        out_shape=jax.ShapeDtypeStruct((m, d), _F32),
        scratch_shapes=[pltpu.VMEM((nk, tm, tk), _BF16)],
        compiler_params=_params(("parallel", "arbitrary")),
        name="out_proj",
    )(oa, ob, oc, ga.reshape(1, wa), gc.reshape(1, wc), w, x, gp.reshape(1, d))


def _mlp_body(x_ref, g1_ref, wu_ref, wd_ref, g2_ref, o_ref, xn_ref, hid0_ref, hid1_ref, *, nf):
    f = pl.program_id(1)
    hid_refs = (hid0_ref, hid1_ref)

    def up(dst_ref):
        hid = jnp.dot(xn_ref[...], wu_ref[...], preferred_element_type=_F32)
        dst_ref[...] = jnp.square(jnp.maximum(hid, 0.0)).astype(_BF16)

    def down(src_ref):
        o_ref[...] += jnp.dot(src_ref[...], wd_ref[...], preferred_element_type=_F32)

    @pl.when(f == 0)
    def _():
        _norm_rows_to(xn_ref, x_ref, g1_ref)
        o_ref[...] = jnp.zeros(o_ref.shape, _F32)
        up(hid_refs[0])

    for parity in range(2):
        @pl.when((f > 0) & (f < nf) & (f % 2 == parity))
        def _(parity=parity):
            up(hid_refs[parity])
            down(hid_refs[1 - parity])

    @pl.when(f == nf)
    def _():
        down(hid_refs[(nf - 1) % 2])
        _residual_norm_rows(o_ref, x_ref, g2_ref)


def _mlp(x, g1, wu, wd, g2):
    m, d = x.shape
    dff = wu.shape[1]
    tm = _pick_tile(m, (512, 256, 128))
    tf = _pick_tile(dff, (512, 256, 128))
    row = pl.BlockSpec((tm, d), lambda i, f: (i, 0))
    vec = pl.BlockSpec((1, d), lambda i, f: (0, 0))
    nf = dff // tf
    return pl.pallas_call(
        functools.partial(_mlp_body, nf=nf),
        grid=(m // tm, nf + 1),
        in_specs=[row, vec,
                  pl.BlockSpec((d, tf), lambda i, f: (0, jnp.minimum(f, nf - 1))),
                  pl.BlockSpec((tf, d), lambda i, f: (jnp.maximum(f - 1, 0), 0)),
                  vec],
        out_specs=row,
        out_shape=jax.ShapeDtypeStruct((m, d), _F32),
        scratch_shapes=[pltpu.VMEM((tm, d), _BF16), pltpu.VMEM((tm, tf), _BF16), pltpu.VMEM((tm, tf), _BF16)],
        compiler_params=_params(("parallel", "arbitrary")),
        name="mlp",
    )(x, g1.reshape(1, d), wu, wd, g2.reshape(1, d))


def kernel(x, rel_bias, pre_mix_norm, w_in, conv_w, q_norm, k_norm, out_norm_a, out_norm_b, out_norm_c,
           w_out, post_mix_norm, pre_mlp_norm, w_up, w_down, post_mlp_norm):
    batch, seq, d = x.shape
    depth = w_in.shape[0]
    wa, wb, wc = out_norm_a.shape[1], out_norm_b.shape[1], out_norm_c.shape[1]
    in_width = w_in.shape[2]
    kv_width = (in_width - 3 * wa - 3 * wb - wc) // 2
    a_heads, q_heads, kv_heads = wa // HEAD_DIM, wc // HEAD_DIM, kv_width // HEAD_DIM
    assert all(w // (2 * dil) == SPAN for w, dil in A_BRANCHES)
    assert rel_bias.shape == (NUM_BUCKETS, a_heads) and seq % GRID_W == 0

    idx_tbl = jnp.asarray(_bucket_table())
    tables = _rope_tables(seq)
    xf = x.reshape(batch * seq, d)
    w_in_b = _cast_bf16(w_in, 0)
    for i in range(depth):
        q0 = 3 * wb
        proj_a, proj_r = _in_proj(xf, pre_mix_norm[i], w_in_b, 3 * wa, jnp.stack([q_norm[i], k_norm[i]]), tables, seq,
                                  (q0, q0 + wc), (q0 + wc, q0 + wc + kv_width))
        oa = _mixer_a(proj_a, rel_bias, idx_tbl, batch, seq, a_heads)
        ob = _mixer_b(proj_r, conv_w[i], out_norm_b[i], seq)
        casts = [(w_out, i), (w_up, i), (w_down, i)] + ([(w_in, i + 1)] if i + 1 < depth else [])
        oc, cast = _attn_c(proj_r, batch, seq, q_heads, kv_heads, q0, casts)
        w_out_b, w_up_b, w_down_b = cast[:3]
        w_in_b = cast[3] if i + 1 < depth else None
        xf = _out_proj(oa, ob, oc, out_norm_a[i], out_norm_c[i], w_out_b, xf, post_mix_norm[i])
        xf = _mlp(xf, pre_mlp_norm[i], w_up_b, w_down_b, post_mlp_norm[i])
    return xf.reshape(batch, seq, d)
```

```python
import functools
import math

import numpy as np
import jax
import jax.numpy as jnp
from jax import lax
from jax.experimental import pallas as pl
from jax.experimental.pallas import tpu as pltpu

HEAD_DIM = 128
A_BRANCHES = ((128, 1), (512, 4), (2048, 16))
SPAN = 64
Q_TILE_A = 2 * SPAN
K_TILE_A = 4 * SPAN
GROUP_A = 8
ROPE_THETA = 10000.0
GRID_W = 64
NUM_BUCKETS = 32
MAX_DISTANCE = 1024
CONV_WIDTH = 3
GQA_GROUP = 4
EPS = 1e-6
NEG = -1e30
Q_SCALE = HEAD_DIM ** -0.5
LOG2_E = math.log2(math.e)

V7X_VMEM_LIMIT_BYTES = 60 * 1024 * 1024
BF16_SUBLANE_TILE = 16

_F32 = jnp.float32
_BF16 = jnp.bfloat16


def _pick_tile(n, prefs):
    for t in prefs:
        if n % t == 0:
            return t
    return n


def _params(sem):
    return pltpu.CompilerParams(dimension_semantics=sem, vmem_limit_bytes=V7X_VMEM_LIMIT_BYTES)


def _row_rms_inv(v):
    return lax.rsqrt(jnp.mean(v * v, axis=-1, keepdims=True) + EPS)


ROW_CHUNK = 128
LANES = 128


def _for_row_chunks(nrows, fn):
    def step(c, carry):
        fn(pl.ds(pl.multiple_of(c * ROW_CHUNK, ROW_CHUNK), ROW_CHUNK))
        return carry

    lax.fori_loop(0, nrows // ROW_CHUNK, step, 0)


def _lane_tiles(width):
    return [slice(c, c + LANES) for c in range(0, width, LANES)]


def _rows_rms_inv(ref, rows):
    width = ref.shape[1]
    acc = None
    for cols in _lane_tiles(width):
        blk = ref[rows, cols].astype(_F32)
        acc = blk * blk if acc is None else acc + blk * blk
    inv = lax.rsqrt(jnp.sum(acc, axis=-1, keepdims=True) / width + EPS)
    return jnp.broadcast_to(inv, acc.shape)


def _norm_rows_to(dst_ref, src_ref, g_ref):
    def fn(rows):
        inv = _rows_rms_inv(src_ref, rows)
        for cols in _lane_tiles(src_ref.shape[1]):
            dst_ref[rows, cols] = (src_ref[rows, cols].astype(_F32) * inv * g_ref[:, cols]).astype(dst_ref.dtype)

    _for_row_chunks(src_ref.shape[0], fn)


def _residual_norm_rows(o_ref, x_ref, g_ref):
    def fn(rows):
        inv = _rows_rms_inv(o_ref, rows)
        for cols in _lane_tiles(o_ref.shape[1]):
            o_ref[rows, cols] = x_ref[rows, cols] + o_ref[rows, cols] * inv * g_ref[:, cols]

    _for_row_chunks(o_ref.shape[0], fn)


CAST_BLOCK_BYTES = 8 * 1024 * 1024


def _cast_body(w_ref, o_ref):
    o_ref[...] = w_ref[...].astype(o_ref.dtype)


def _cast_bf16(w, layer):
    _, k, n = w.shape
    tr = _pick_tile(k, [t for t in (2048, 1024, 512, 256, 128, 64, 32, 16) if t * n * 4 <= CAST_BLOCK_BYTES])
    return pl.pallas_call(
        _cast_body,
        grid=(k // tr,),
        in_specs=[pl.BlockSpec((None, tr, n), lambda r: (layer, r, 0))],
        out_specs=pl.BlockSpec((tr, n), lambda r: (r, 0)),
        out_shape=jax.ShapeDtypeStruct((k, n), _BF16),
        compiler_params=_params(("parallel",)),
        name="cast_bf16",
    )(w)


def _rope_head(v, gain, cos, sin_hi, sin_lo, scale):
    quarter = HEAD_DIM // 4
    lanes = v.ndim - 1
    xn = v * (_row_rms_inv(v) * scale) * gain
    return xn * cos + pltpu.roll(xn, HEAD_DIM - quarter, lanes) * sin_hi + pltpu.roll(xn, quarter, lanes) * sin_lo


def _in_proj_body(x_ref, g_ref, w_ref, hg_ref, cos_ref, shi_ref, slo_ref, oa_ref, or_ref, xn_ref, *, n_a, kinds):
    j = pl.program_id(1)

    @pl.when(j == 0)
    def _():
        _norm_rows_to(xn_ref, x_ref, g_ref)

    @pl.when(j < n_a)
    def _():
        oa_ref[...] = jnp.dot(xn_ref[...], w_ref[...], preferred_element_type=_F32)

    for sig in sorted(set(kinds)):
        tiles = [t for t, k in enumerate(kinds) if k == sig]

        @pl.when(functools.reduce(jnp.logical_or, [j == n_a + t for t in tiles]))
        def _(sig=sig):
            if all(kind == "p" for kind in sig):
                or_ref[...] = jnp.dot(xn_ref[...], w_ref[...], preferred_element_type=_F32).astype(_BF16)
                return
            res = jnp.dot(xn_ref[...], w_ref[...], preferred_element_type=_F32)
            head = lambda s: res[:, s * HEAD_DIM:(s + 1) * HEAD_DIM]
            for kind, gain_row, scale in (("q", 0, Q_SCALE * LOG2_E), ("k", 1, 1.0)):
                slots = [s for s, k in enumerate(sig) if k == kind]
                if slots:
                    v = jnp.stack([head(s) for s in slots])
                    v = _rope_head(v, hg_ref[gain_row:gain_row + 1, :], cos_ref[...], shi_ref[...], slo_ref[...], scale)
                    for n, s in enumerate(slots):
                        or_ref[:, s * HEAD_DIM:(s + 1) * HEAD_DIM] = v[n].astype(_BF16)
            for s, k in enumerate(sig):
                if k == "p":
                    or_ref[:, s * HEAD_DIM:(s + 1) * HEAD_DIM] = head(s).astype(_BF16)


def _in_proj(x, g, w, a_cols, head_gains, tables, seq, q_cols, k_cols):
    m, d = x.shape
    n = w.shape[1]
    tm = _pick_tile(seq, (512, 256, 128))
    tn = next(t for t in (1024, 512, 256, 128) if a_cols % t == 0 and (n - a_cols) % t == 0)
    n_a = a_cols // tn

    def kind(col):
        return "q" if q_cols[0] <= col < q_cols[1] else "k" if k_cols[0] <= col < k_cols[1] else "p"

    kinds = tuple(tuple(kind(t * tn + s * HEAD_DIM) for s in range(tn // HEAD_DIM)) for t in range((n - a_cols) // tn))
    per_seq = seq // tm
    tab = pl.BlockSpec((tm, HEAD_DIM), lambda i, j: (i % per_seq, 0))
    return pl.pallas_call(
        functools.partial(_in_proj_body, n_a=n_a, kinds=kinds),
        grid=(m // tm, n // tn),
        in_specs=[
            pl.BlockSpec((tm, d), lambda i, j: (i, 0)),
            pl.BlockSpec((1, d), lambda i, j: (0, 0)),
            pl.BlockSpec((d, tn), lambda i, j: (0, j)),
            pl.BlockSpec((2, HEAD_DIM), lambda i, j: (0, 0)),
            tab, tab, tab,
        ],
        out_specs=[
            pl.BlockSpec((tm, tn), lambda i, j: (i, jnp.minimum(j, n_a - 1))),
            pl.BlockSpec((tm, tn), lambda i, j: (i, jnp.maximum(j - n_a, 0))),
        ],
        out_shape=[
            jax.ShapeDtypeStruct((m, a_cols), _F32),
            jax.ShapeDtypeStruct((m, n - a_cols), _BF16),
        ],
        scratch_shapes=[pltpu.VMEM((tm, d), _BF16)],
        compiler_params=_params(("parallel", "arbitrary")),
        name="in_proj",
    )(x, g.reshape(1, d), w, head_gains, *tables)


def _bucket_table():
    half = NUM_BUCKETS // 2
    max_exact = half // 2
    i = np.arange(Q_TILE_A)[:, None]
    c = np.arange(K_TILE_A)[None, :]
    out = np.zeros((len(A_BRANCHES), 3, Q_TILE_A, K_TILE_A), np.int32)
    for br, (_, dil) in enumerate(A_BRANCHES):
        for var in range(3):
            rel = c - SPAN * var - i
            dist = rel * dil
            n = np.abs(dist)
            t = np.log(np.maximum(n, 1) / max_exact) / math.log(MAX_DISTANCE / max_exact) * (half - max_exact)
            valid = np.abs(rel) <= SPAN
            frac = np.abs(t - np.round(t))
            assert np.all((frac > 1e-4) | (n <= max_exact) | (n >= MAX_DISTANCE) | ~valid), "bucket edge near an integer"
            large = np.minimum(max_exact + t.astype(np.int32), half - 1)
            bucket = np.where(dist > 0, half, 0) + np.where(n < max_exact, n, large)
            out[br, var] = np.where(valid, bucket, NUM_BUCKETS)
    return out


def _mixer_a_body(rel_ref, idx_ref, q_ref, k_ref, v_ref, o_ref, bias_ref, acc_ref, m_ref, l_ref, *, seq):
    h = pl.program_id(1)

    for br in range(len(A_BRANCHES)):
        for var in range(3):
            idx = idx_ref[br, var]

            def fill(bkt, bias, idx=idx):
                return jnp.where(idx == bkt, rel_ref[bkt, h], bias)

            bias_ref[br, var] = lax.fori_loop(0, NUM_BUCKETS, fill, jnp.full(idx.shape, NEG, _F32))

    order = sorted(range(len(A_BRANCHES)), key=lambda b: -A_BRANCHES[b][1])
    for br in order:
        dil = A_BRANCHES[br][1]
        length = seq // dil
        nblk = length // Q_TILE_A
        first = br == order[0]

        def rows(start, size, dil=dil):
            if dil == 1:
                return pl.ds(pl.multiple_of(start, SPAN), size)
            return pl.ds(start, size, stride=dil)

        def group(blocks, br=br, dil=dil, length=length, nblk=nblk, first=first, rows=rows):
            qsls, ss, vs = [], [], []
            whole = {}
            for n, r in blocks:
                p0 = n * Q_TILE_A
                qsl = rows(r + dil * p0, Q_TILE_A)
                q = (q_ref[qsl, :] * Q_SCALE).astype(_BF16)
                if isinstance(n, int):
                    ks = min(max(p0 - SPAN, 0), length - K_TILE_A)
                    var = 0 if n == 0 else 2 if n == nblk - 1 else 1
                    if id(r) not in whole:
                        seq_rows = rows(r, length)
                        whole[id(r)] = (k_ref[seq_rows, :].astype(_BF16), v_ref[seq_rows, :].astype(_BF16))
                    k, v = (t[ks:ks + K_TILE_A] for t in whole[id(r)])
                else:
                    ks = jnp.clip(p0 - SPAN, 0, length - K_TILE_A)
                    var = jnp.where(n == 0, 0, jnp.where(n == nblk - 1, 2, 1))
                    ksl = rows(r + dil * ks, K_TILE_A)
                    k, v = k_ref[ksl, :].astype(_BF16), v_ref[ksl, :].astype(_BF16)
                s = lax.dot_general(q, k, (((1,), (1,)), ((), ())), preferred_element_type=_F32)
                qsls.append(qsl)
                ss.append(s + bias_ref[br, var])
                vs.append(v)
            s = jnp.stack(ss)
            m_blk = jnp.max(s, axis=-1, keepdims=True)
            if first:
                m_new = m_blk
                p = jnp.exp(s - m_new)
            else:
                m_old = jnp.stack([m_ref[qsl, :] for qsl in qsls])
                m_new = jnp.maximum(m_old, m_blk)
                p = jnp.exp(s - jnp.concatenate([m_new] * (K_TILE_A // HEAD_DIM), axis=-1))
            l_blk = jnp.sum(p, axis=-1, keepdims=True)
            p = p.astype(_BF16)
            pv = jnp.stack([jnp.dot(p[j], vs[j], preferred_element_type=_F32) for j in range(GROUP_A)])
            if first:
                acc_new = pv
                l_new = jnp.broadcast_to(l_blk, pv.shape)
                m_new = jnp.broadcast_to(m_new, pv.shape)
            else:
                alpha = jnp.exp(m_old - m_new)
                acc_new = alpha * jnp.stack([acc_ref[qsl, :] for qsl in qsls]) + pv
                l_new = alpha * jnp.stack([l_ref[qsl, :] for qsl in qsls]) + l_blk
            for j, qsl in enumerate(qsls):
                acc_ref[qsl, :] = acc_new[j]
                l_ref[qsl, :] = l_new[j]
                m_ref[qsl, :] = m_new[j]

        if nblk <= GROUP_A:
            res_per_group = GROUP_A // nblk

            def body(rg, carry, group=group, res_per_group=res_per_group, nblk=nblk):
                residues = [rg * res_per_group + jr for jr in range(res_per_group)]
                group([(n, r) for r in residues for n in range(nblk)])
                return carry

            lax.fori_loop(0, dil // res_per_group, body, 0)
        else:
            res_per_group = min(dil, GROUP_A)
            blk_per_group = GROUP_A // res_per_group
            ngrp_n = nblk // blk_per_group

            def body(it, carry, group=group, res_per_group=res_per_group, blk_per_group=blk_per_group, ngrp_n=ngrp_n):
                rg, ng = it // ngrp_n, it % ngrp_n
                group([(ng * blk_per_group + jn, rg * res_per_group + jr)
                       for jn in range(blk_per_group) for jr in range(res_per_group)])
                return carry

            lax.fori_loop(0, (dil // res_per_group) * ngrp_n, body, 0)

    chunk = _pick_tile(seq, (512, 256, 128))

    def finish(c, carry):
        sl = pl.ds(pl.multiple_of(c * chunk, chunk), chunk)
        o_ref[sl, :] = (acc_ref[sl, :] / l_ref[sl, :]).astype(o_ref.dtype)
        return carry

    lax.fori_loop(0, seq // chunk, finish, 0)


def _mixer_a(proj_a, rel_bias, idx_tbl, batch, seq, heads):
    m = proj_a.shape[0]
    assert seq % (A_BRANCHES[-1][1] * K_TILE_A) == 0, "sequence too short for the widest dilation"
    for _, dil in A_BRANCHES:
        nblk = seq // (dil * Q_TILE_A)
        if nblk <= GROUP_A:
            assert GROUP_A % nblk == 0 and dil % (GROUP_A // nblk) == 0
        else:
            assert GROUP_A % min(dil, GROUP_A) == 0 and nblk % (GROUP_A // min(dil, GROUP_A)) == 0
    blk = (seq, HEAD_DIM)
    return pl.pallas_call(
        functools.partial(_mixer_a_body, seq=seq),
        grid=(batch, heads),
        in_specs=[
            pl.BlockSpec(memory_space=pltpu.SMEM),
            pl.BlockSpec(idx_tbl.shape, lambda b, h: (0, 0, 0, 0)),
            pl.BlockSpec(blk, lambda b, h: (b, h)),
            pl.BlockSpec(blk, lambda b, h: (b, heads + h)),
            pl.BlockSpec(blk, lambda b, h: (b, 2 * heads + h)),
        ],
        out_specs=pl.BlockSpec(blk, lambda b, h: (b, h)),
        out_shape=jax.ShapeDtypeStruct((m, heads * HEAD_DIM), _BF16),
        scratch_shapes=[
            pltpu.VMEM(idx_tbl.shape, _F32),
            pltpu.VMEM(blk, _F32),
            pltpu.VMEM(blk, _F32),
            pltpu.VMEM(blk, _F32),
        ],
        compiler_params=_params(("parallel", "parallel")),
        name="mixer_a",
    )(rel_bias, idx_tbl, proj_a, proj_a, proj_a)


def _mixer_b_body(gb_ref, gc_ref, hb_ref, gcp_ref, hbp_ref, gcn_ref, hbn_ref, w_ref, g_ref, o_ref, *, tiles_per_seq):
    i = pl.program_id(0)
    ts = gb_ref.shape[0]
    pos = i % tiles_per_seq
    u = gc_ref[...].astype(_F32) * hb_ref[...].astype(_F32)
    last = BF16_SUBLANE_TILE - 1
    u_prev = gcp_ref[last:last + 1, :].astype(_F32) * hbp_ref[last:last + 1, :].astype(_F32)
    u_next = gcn_ref[0:1, :].astype(_F32) * hbn_ref[0:1, :].astype(_F32)
    u_prev = jnp.where(pos == 0, 0.0, u_prev)
    u_next = jnp.where(pos == tiles_per_seq - 1, 0.0, u_next)
    row = lax.broadcasted_iota(jnp.int32, (ts, 1), 0)
    up = jnp.where(row == 0, u_prev, pltpu.roll(u, 1, 0))
    un = jnp.where(row == ts - 1, u_next, pltpu.roll(u, ts - 1, 0))
    y = gb_ref[...].astype(_F32) * (w_ref[0:1, :] * up + w_ref[1:2, :] * u + w_ref[2:3, :] * un)
    o_ref[...] = (y * _row_rms_inv(y) * g_ref[...]).astype(o_ref.dtype)


def _mixer_b(proj_r, conv_w, gain, seq):
    m = proj_r.shape[0]
    wb = gain.shape[0]
    ts = _pick_tile(seq, (512, 256, 128))
    halo = BF16_SUBLANE_TILE
    per = ts // halo
    nhalo = m // halo
    main = lambda c: pl.BlockSpec((ts, wb), lambda i: (i, c))
    prev = lambda c: pl.BlockSpec((halo, wb), lambda i: (jnp.maximum(i * per - 1, 0), c))
    nxt = lambda c: pl.BlockSpec((halo, wb), lambda i: (jnp.minimum((i + 1) * per, nhalo - 1), c))
    return pl.pallas_call(
        functools.partial(_mixer_b_body, tiles_per_seq=seq // ts),
        grid=(m // ts,),
        in_specs=[main(0), main(1), main(2), prev(1), prev(2), nxt(1), nxt(2),
                  pl.BlockSpec((CONV_WIDTH, wb), lambda i: (0, 0)),
                  pl.BlockSpec((1, wb), lambda i: (0, 0))],
        out_specs=pl.BlockSpec((ts, wb), lambda i: (i, 0)),
        out_shape=jax.ShapeDtypeStruct((m, wb), _BF16),
        compiler_params=_params(("parallel",)),
        name="mixer_b",
    )(proj_r, proj_r, proj_r, proj_r, proj_r, proj_r, proj_r, conv_w, gain.reshape(1, wb))


def _rope_tables(seq):
    quarter = HEAD_DIM // 4
    inv = ROPE_THETA ** (-jnp.arange(quarter, dtype=_F32) / quarter)
    t = jnp.arange(seq)
    ang_r = (t // GRID_W).astype(_F32)[:, None] * inv[None, :]
    ang_c = (t % GRID_W).astype(_F32)[:, None] * inv[None, :]
    zero = jnp.zeros_like(ang_r)
    cos = jnp.concatenate([jnp.cos(ang_r), jnp.cos(ang_r), jnp.cos(ang_c), jnp.cos(ang_c)], axis=-1)
    sin_hi = jnp.concatenate([-jnp.sin(ang_r), zero, -jnp.sin(ang_c), zero], axis=-1)
    sin_lo = jnp.concatenate([zero, jnp.sin(ang_r), zero, jnp.sin(ang_c)], axis=-1)
    return cos, sin_hi, sin_lo


def _attn_c_body(q_ref, k_ref, v_ref, *rest, tk, ncast):
    w_refs, o_ref, wo_refs = rest[:ncast], rest[ncast], rest[ncast + 1:2 * ncast + 1]
    vt_ref, qt_ref, acc_ref, *bufs = rest[2 * ncast + 1:]
    nchunk = vt_ref.shape[0]
    ntile = qt_ref.shape[0]
    tq = q_ref.shape[0] // ntile
    nq = GQA_GROUP * tq

    @pl.when(pl.program_id(2) == 0)
    def _():
        def transpose_v(c, carry):
            rows = pl.ds(pl.multiple_of(c * tk, tk), tk)
            vt_ref[c, 0:HEAD_DIM, :] = v_ref[rows, :].astype(_F32).T.astype(_BF16)
            vt_ref[c, HEAD_DIM:, :] = jnp.ones((BF16_SUBLANE_TILE, tk), _BF16)
            return carry

        lax.fori_loop(0, nchunk, transpose_v, 0)

    for t in range(ntile):
        for g in range(GQA_GROUP):
            q = q_ref[t * tq:(t + 1) * tq, g * HEAD_DIM:(g + 1) * HEAD_DIM]
            qt_ref[t, :, g * tq:(g + 1) * tq] = q.astype(_F32).T.astype(_BF16)
    acc_ref[...] = jnp.zeros(acc_ref.shape, _F32)
    for w_ref, wo_ref in zip(w_refs, wo_refs):
        wo_ref[...] = w_ref[...].astype(wo_ref.dtype)

    s_sets = (bufs[0:2], bufs[2:4])
    p_sets = (bufs[4:6], bufs[6:8])
    npair = nchunk // 2
    nitem = ntile * npair

    def stage_scores(item, s_set):
        tile, pair = item // npair, item % npair
        out = []
        for j in range(2):
            rows = pl.ds(pl.multiple_of((2 * pair + j) * tk, tk), tk)
            s = jnp.dot(k_ref[rows, :], qt_ref[tile], preferred_element_type=_F32)
            s_set[j][...] = s
            out.append(jnp.max(s, axis=0, keepdims=True))
        return tuple(out)

    def stage_exp(item, s_set, p_set, maxes, m_run):
        m_run = jnp.where(item % npair == 0, NEG, m_run)
        alphas = []
        for j in range(2):
            m_new = jnp.maximum(m_run, maxes[j])
            alphas.append(jnp.exp2(m_run - m_new))
            p_set[j][...] = jnp.exp2((s_set[j][...] - m_new).astype(_BF16))
            m_run = m_new
        return tuple(alphas), m_run

    def stage_pv(item, p_set, alphas):
        tile, pair = item // npair, item % npair
        for j in range(2):
            acc_ref[tile] = alphas[j] * acc_ref[tile] + jnp.dot(vt_ref[2 * pair + j], p_set[j][...],
                                                                 preferred_element_type=_F32)

    m_run = jnp.full((1, nq), NEG, _F32)
    mx0 = stage_scores(0, s_sets[0])
    mx1 = stage_scores(1, s_sets[1])
    al0, m_run = stage_exp(0, s_sets[0], p_sets[0], mx0, m_run)

    def two_steps(it, carry):
        m_run, mx1, al0 = carry
        j = 2 * it
        mx0 = stage_scores(j + 2, s_sets[0])
        al1, m_run = stage_exp(j + 1, s_sets[1], p_sets[1], mx1, m_run)
        stage_pv(j, p_sets[0], al0)
        mx1 = stage_scores(j + 3, s_sets[1])
        al0, m_run = stage_exp(j + 2, s_sets[0], p_sets[0], mx0, m_run)
        stage_pv(j + 1, p_sets[1], al1)
        return m_run, mx1, al0

    m_run, mx1, al0 = lax.fori_loop(0, nitem // 2 - 1, two_steps, (m_run, mx1, al0))
    al1, m_run = stage_exp(nitem - 1, s_sets[1], p_sets[1], mx1, m_run)
    stage_pv(nitem - 2, p_sets[0], al0)
    stage_pv(nitem - 1, p_sets[1], al1)
    for t in range(ntile):
        out_t = acc_ref[t, 0:HEAD_DIM, :] / acc_ref[t, HEAD_DIM:HEAD_DIM + 1, :]
        for g in range(GQA_GROUP):
            o_ref[t * tq:(t + 1) * tq, g * HEAD_DIM:(g + 1) * HEAD_DIM] = (
                out_t[:, g * tq:(g + 1) * tq].T.astype(o_ref.dtype))


def _attn_c(proj_r, batch, seq, q_heads, kv_heads, q_col0, casts):
    m = proj_r.shape[0]
    assert q_heads == GQA_GROUP * kv_heads and q_col0 % (GQA_GROUP * HEAD_DIM) == 0
    q_blk0 = q_col0 // (GQA_GROUP * HEAD_DIM)
    k_blk0 = q_col0 // HEAD_DIM + q_heads
    v_blk0 = k_blk0 + kv_heads
    tq = _pick_tile(seq, (256, 128))
    ntile = next(t for t in (2, 1) if seq % (t * tq) == 0)
    tk = _pick_tile(seq, (512, 256, 128))
    assert (seq // tk) % 4 == 0, "the chunk pipeline advances two pairs of key chunks per loop step"
    tq_step = ntile * tq
    nq = seq // tq_step
    gw = GQA_GROUP * HEAD_DIM
    nsteps = batch * kv_heads * nq
    step = lambda b, g, i: (b * kv_heads + g) * nq + i
    cast_in, cast_out, cast_shapes = [], [], []
    for w, layer in casts:
        _, k, n = w.shape
        assert k % (nsteps * BF16_SUBLANE_TILE) == 0
        rows = k // nsteps
        cast_in.append(pl.BlockSpec((None, rows, n), lambda b, g, i, layer=layer: (layer, step(b, g, i), 0)))
        cast_out.append(pl.BlockSpec((rows, n), lambda b, g, i: (step(b, g, i), 0)))
        cast_shapes.append(jax.ShapeDtypeStruct((k, n), _BF16))
    outs = pl.pallas_call(
        functools.partial(_attn_c_body, tk=tk, ncast=len(casts)),
        grid=(batch, kv_heads, nq),
        in_specs=[pl.BlockSpec((tq_step, gw), lambda b, g, i: (b * nq + i, q_blk0 + g)),
                  pl.BlockSpec((seq, HEAD_DIM), lambda b, g, i: (b, k_blk0 + g)),
                  pl.BlockSpec((seq, HEAD_DIM), lambda b, g, i: (b, v_blk0 + g))] + cast_in,
        out_specs=[pl.BlockSpec((tq_step, gw), lambda b, g, i: (b * nq + i, g))] + cast_out,
        out_shape=[jax.ShapeDtypeStruct((m, q_heads * HEAD_DIM), _BF16)] + cast_shapes,
        scratch_shapes=[pltpu.VMEM((seq // tk, HEAD_DIM + BF16_SUBLANE_TILE, tk), _BF16),
                        pltpu.VMEM((ntile, HEAD_DIM, GQA_GROUP * tq), _BF16),
                        pltpu.VMEM((ntile, HEAD_DIM + BF16_SUBLANE_TILE, GQA_GROUP * tq), _F32)]
        + [pltpu.VMEM((tk, GQA_GROUP * tq), _F32)] * 4
        + [pltpu.VMEM((tk, GQA_GROUP * tq), _BF16)] * 4,
        compiler_params=_params(("arbitrary", "arbitrary", "arbitrary")),
        name="attn_c",
    )(proj_r, proj_r, proj_r, *[w for w, _ in casts])
    return outs[0], outs[1:]


def _out_proj_body(oa_ref, ob_ref, oc_ref, ga_ref, gc_ref, w_ref, x_ref, gp_ref, o_ref, xn_ref, *, tk):
    k = pl.program_id(1)
    nk = pl.num_programs(1)

    @pl.when(k == 0)
    def _():
        def fn(rows):
            c0 = 0
            for ref, g_ref in ((oa_ref, ga_ref), (ob_ref, None), (oc_ref, gc_ref)):
                inv = None if g_ref is None else _rows_rms_inv(ref, rows)
                for cols in _lane_tiles(ref.shape[1]):
                    val = ref[rows, cols]
                    if g_ref is not None:
                        val = (val.astype(_F32) * inv * g_ref[:, cols]).astype(_BF16)
                    xn_ref[c0 + cols.start // tk, rows, cols.start % tk:cols.start % tk + LANES] = val
                c0 += ref.shape[1] // tk

        _for_row_chunks(oa_ref.shape[0], fn)
        o_ref[...] = jnp.dot(xn_ref[0], w_ref[...], preferred_element_type=_F32)

    @pl.when(k > 0)
    def _():
        o_ref[...] += jnp.dot(xn_ref[k], w_ref[...], preferred_element_type=_F32)

    @pl.when(k == nk - 1)
    def _():
        _residual_norm_rows(o_ref, x_ref, gp_ref)


def _out_proj(oa, ob, oc, ga, gc, w, x, gp):
    m, d = x.shape
    kdim = w.shape[0]
    wa, wb, wc = oa.shape[1], ob.shape[1], oc.shape[1]
    tm = _pick_tile(m, (512, 256, 128))
    tk = next(t for t in (512, 256, 128) if wa % t == 0 and wb % t == 0 and wc % t == 0)
    nk = kdim // tk
    row = lambda width: pl.BlockSpec((tm, width), lambda i, k: (i, 0))
    vec = lambda width: pl.BlockSpec((1, width), lambda i, k: (0, 0))
    return pl.pallas_call(
        functools.partial(_out_proj_body, tk=tk),
        grid=(m // tm, nk),
        in_specs=[row(wa), row(wb), row(wc), vec(wa), vec(wc),
                  pl.BlockSpec((tk, d), lambda i, k: (k, 0)),
                  row(d), vec(d)],
        out_specs=row(d),
        out_shape=jax.ShapeDtypeStruct((m, d), _F32),
        scratch_shapes=[pltpu.VMEM((nk, tm, tk), _BF16)],
        compiler_params=_params(("parallel", "arbitrary")),
        name="out_proj",
    )(oa, ob, oc, ga.reshape(1, wa), gc.reshape(1, wc), w, x, gp.reshape(1, d))


def _mlp_body(x_ref, g1_ref, wu_ref, wd_ref, g2_ref, o_ref, xn_ref, hid0_ref, hid1_ref, *, nf):
    f = pl.program_id(1)
    hid_refs = (hid0_ref, hid1_ref)

    def up(dst_ref):
        hid = jnp.dot(xn_ref[...], wu_ref[...], preferred_element_type=_F32)
        dst_ref[...] = jnp.square(jnp.maximum(hid, 0.0)).astype(_BF16)

    def down(src_ref):
        o_ref[...] += jnp.dot(src_ref[...], wd_ref[...], preferred_element_type=_F32)

    @pl.when(f == 0)
    def _():
        _norm_rows_to(xn_ref, x_ref, g1_ref)
        o_ref[...] = jnp.zeros(o_ref.shape, _F32)
        up(hid_refs[0])

    for parity in range(2):
        @pl.when((f > 0) & (f < nf) & (f % 2 == parity))
        def _(parity=parity):
            up(hid_refs[parity])
            down(hid_refs[1 - parity])

    @pl.when(f == nf)
    def _():
        down(hid_refs[(nf - 1) % 2])
        _residual_norm_rows(o_ref, x_ref, g2_ref)


def _mlp(x, g1, wu, wd, g2):
    m, d = x.shape
    dff = wu.shape[1]
    tm = _pick_tile(m, (512, 256, 128))
    tf = _pick_tile(dff, (512, 256, 128))
    row = pl.BlockSpec((tm, d), lambda i, f: (i, 0))
    vec = pl.BlockSpec((1, d), lambda i, f: (0, 0))
    nf = dff // tf
    return pl.pallas_call(
        functools.partial(_mlp_body, nf=nf),
        grid=(m // tm, nf + 1),
        in_specs=[row, vec,
                  pl.BlockSpec((d, tf), lambda i, f: (0, jnp.minimum(f, nf - 1))),
                  pl.BlockSpec((tf, d), lambda i, f: (jnp.maximum(f - 1, 0), 0)),
                  vec],
        out_specs=row,
        out_shape=jax.ShapeDtypeStruct((m, d), _F32),
        scratch_shapes=[pltpu.VMEM((tm, d), _BF16), pltpu.VMEM((tm, tf), _BF16), pltpu.VMEM((tm, tf), _BF16)],
        compiler_params=_params(("parallel", "arbitrary")),
        name="mlp",
    )(x, g1.reshape(1, d), wu, wd, g2.reshape(1, d))


def kernel(x, rel_bias, pre_mix_norm, w_in, conv_w, q_norm, k_norm, out_norm_a, out_norm_b, out_norm_c,
           w_out, post_mix_norm, pre_mlp_norm, w_up, w_down, post_mlp_norm):
    batch, seq, d = x.shape
    depth = w_in.shape[0]
    wa, wb, wc = out_norm_a.shape[1], out_norm_b.shape[1], out_norm_c.shape[1]
    in_width = w_in.shape[2]
    kv_width = (in_width - 3 * wa - 3 * wb - wc) // 2
    a_heads, q_heads, kv_heads = wa // HEAD_DIM, wc // HEAD_DIM, kv_width // HEAD_DIM
    assert all(w // (2 * dil) == SPAN for w, dil in A_BRANCHES)
    assert rel_bias.shape == (NUM_BUCKETS, a_heads) and seq % GRID_W == 0

    idx_tbl = jnp.asarray(_bucket_table())
    tables = _rope_tables(seq)
    xf = x.reshape(batch * seq, d)
    w_in_b = _cast_bf16(w_in, 0)
    for i in range(depth):
        q0 = 3 * wb
        proj_a, proj_r = _in_proj(xf, pre_mix_norm[i], w_in_b, 3 * wa, jnp.stack([q_norm[i], k_norm[i]]), tables, seq,
                                  (q0, q0 + wc), (q0 + wc, q0 + wc + kv_width))
        oa = _mixer_a(proj_a, rel_bias, idx_tbl, batch, seq, a_heads)
        ob = _mixer_b(proj_r, conv_w[i], out_norm_b[i], seq)
        casts = [(w_out, i), (w_up, i), (w_down, i)] + ([(w_in, i + 1)] if i + 1 < depth else [])
        oc, cast = _attn_c(proj_r, batch, seq, q_heads, kv_heads, q0, casts)
        w_out_b, w_up_b, w_down_b = cast[:3]
        w_in_b = cast[3] if i + 1 < depth else None
        xf = _out_proj(oa, ob, oc, out_norm_a[i], out_norm_c[i], w_out_b, xf, post_mix_norm[i])
        xf = _mlp(xf, pre_mlp_norm[i], w_up_b, w_down_b, post_mlp_norm[i])
    return xf.reshape(batch, seq, d)
```

```python
import functools
import math

import numpy as np
import jax
import jax.numpy as jnp
from jax import lax
from jax.experimental import pallas as pl
from jax.experimental.pallas import tpu as pltpu

HEAD_DIM = 128
A_BRANCHES = ((128, 1), (512, 4), (2048, 16))
SPAN = 64
Q_TILE_A = 2 * SPAN
K_TILE_A = 4 * SPAN
GROUP_A = 8
ROPE_THETA = 10000.0
GRID_W = 64
NUM_BUCKETS = 32
MAX_DISTANCE = 1024
CONV_WIDTH = 3
GQA_GROUP = 4
EPS = 1e-6
NEG = -1e30
Q_SCALE = HEAD_DIM ** -0.5
LOG2_E = math.log2(math.e)

V7X_VMEM_LIMIT_BYTES = 60 * 1024 * 1024
BF16_SUBLANE_TILE = 16

_F32 = jnp.float32
_BF16 = jnp.bfloat16


def _pick_tile(n, prefs):
    for t in prefs:
        if n % t == 0:
            return t
    return n


def _params(sem):
    return pltpu.CompilerParams(dimension_semantics=sem, vmem_limit_bytes=V7X_VMEM_LIMIT_BYTES)


def _row_rms_inv(v):
    return lax.rsqrt(jnp.mean(v * v, axis=-1, keepdims=True) + EPS)


ROW_CHUNK = 128
LANES = 128


def _for_row_chunks(nrows, fn):
    def step(c, carry):
        fn(pl.ds(pl.multiple_of(c * ROW_CHUNK, ROW_CHUNK), ROW_CHUNK))
        return carry

    lax.fori_loop(0, nrows // ROW_CHUNK, step, 0)


def _lane_tiles(width):
    return [slice(c, c + LANES) for c in range(0, width, LANES)]


def _rows_rms_inv(ref, rows):
    width = ref.shape[1]
    acc = None
    for cols in _lane_tiles(width):
        blk = ref[rows, cols].astype(_F32)
        acc = blk * blk if acc is None else acc + blk * blk
    inv = lax.rsqrt(jnp.sum(acc, axis=-1, keepdims=True) / width + EPS)
    return jnp.broadcast_to(inv, acc.shape)


def _norm_rows_to(dst_ref, src_ref, g_ref):
    def fn(rows):
        inv = _rows_rms_inv(src_ref, rows)
        for cols in _lane_tiles(src_ref.shape[1]):
            dst_ref[rows, cols] = (src_ref[rows, cols].astype(_F32) * inv * g_ref[:, cols]).astype(dst_ref.dtype)

    _for_row_chunks(src_ref.shape[0], fn)


def _residual_norm_rows(o_ref, x_ref, g_ref):
    def fn(rows):
        inv = _rows_rms_inv(o_ref, rows)
        for cols in _lane_tiles(o_ref.shape[1]):
            o_ref[rows, cols] = x_ref[rows, cols] + o_ref[rows, cols] * inv * g_ref[:, cols]

    _for_row_chunks(o_ref.shape[0], fn)


CAST_BLOCK_BYTES = 8 * 1024 * 1024


def _cast_body(w_ref, o_ref):
    o_ref[...] = w_ref[...].astype(o_ref.dtype)


def _cast_streams(casts, grid, csplit):
    n0, n1 = grid
    nsteps = n0 * n1
    ins, outs, shapes = [], [], []
    for w, layer in casts:
        _, k, n = w.shape
        groups = k // BF16_SUBLANE_TILE
        nrow = max(t for t in range(1, groups + 1) if groups % t == 0 and t * csplit <= nsteps)
        assert n % (csplit * LANES) == 0
        nslab = nrow * csplit

        def index(i, j, nslab=nslab):
            slab = ((i * n1 + j) * nslab) // nsteps
            return slab // csplit, slab % csplit

        ins.append(pl.BlockSpec((None, k // nrow, n // csplit), lambda i, j, layer=layer, index=index: (layer, *index(i, j))))
        outs.append(pl.BlockSpec((k // nrow, n // csplit), index))
        shapes.append(jax.ShapeDtypeStruct((k, n), _BF16))
    return ins, outs, shapes


def _cast_slabs(w_refs, wo_refs):
    for w_ref, wo_ref in zip(w_refs, wo_refs):
        wo_ref[...] = w_ref[...].astype(wo_ref.dtype)


def _cast_bf16(w, layer):
    _, k, n = w.shape
    tr = _pick_tile(k, [t for t in (2048, 1024, 512, 256, 128, 64, 32, 16) if t * n * 4 <= CAST_BLOCK_BYTES])
    return pl.pallas_call(
        _cast_body,
        grid=(k // tr,),
        in_specs=[pl.BlockSpec((None, tr, n), lambda r: (layer, r, 0))],
        out_specs=pl.BlockSpec((tr, n), lambda r: (r, 0)),
        out_shape=jax.ShapeDtypeStruct((k, n), _BF16),
        compiler_params=_params(("parallel",)),
        name="cast_bf16",
    )(w)


def _rope_head(v, gain, cos, sin_hi, sin_lo, scale):
    quarter = HEAD_DIM // 4
    lanes = v.ndim - 1
    xn = v * (_row_rms_inv(v) * scale) * gain
    return xn * cos + pltpu.roll(xn, HEAD_DIM - quarter, lanes) * sin_hi + pltpu.roll(xn, quarter, lanes) * sin_lo


def _in_proj_body(x_ref, g_ref, w_ref, hg_ref, cos_ref, shi_ref, slo_ref, *rest, n_a, kinds, ncast):
    (oa_ref, or_ref), xn_ref = rest[ncast:ncast + 2], rest[-1]
    cast_slabs = functools.partial(_cast_slabs, rest[:ncast], rest[ncast + 2:-1])
    j = pl.program_id(1)

    @pl.when(j == 0)
    def _():
        _norm_rows_to(xn_ref, x_ref, g_ref)

    @pl.when(j < n_a)
    def _():
        cast_slabs()
        oa_ref[...] = jnp.dot(xn_ref[...], w_ref[...], preferred_element_type=_F32)

    for sig in sorted(set(kinds)):
        tiles = [t for t, k in enumerate(kinds) if k == sig]

        @pl.when(functools.reduce(jnp.logical_or, [j == n_a + t for t in tiles]))
        def _(sig=sig):
            cast_slabs()
            if all(kind == "p" for kind in sig):
                or_ref[...] = jnp.dot(xn_ref[...], w_ref[...], preferred_element_type=_F32).astype(_BF16)
                return
            res = jnp.dot(xn_ref[...], w_ref[...], preferred_element_type=_F32)
            head = lambda s: res[:, s * HEAD_DIM:(s + 1) * HEAD_DIM]
            for kind, gain_row, scale in (("q", 0, Q_SCALE * LOG2_E), ("k", 1, 1.0)):
                slots = [s for s, k in enumerate(sig) if k == kind]
                if slots:
                    v = jnp.stack([head(s) for s in slots])
                    v = _rope_head(v, hg_ref[gain_row:gain_row + 1, :], cos_ref[...], shi_ref[...], slo_ref[...], scale)
                    for n, s in enumerate(slots):
                        or_ref[:, s * HEAD_DIM:(s + 1) * HEAD_DIM] = v[n].astype(_BF16)
            for s, k in enumerate(sig):
                if k == "p":
                    or_ref[:, s * HEAD_DIM:(s + 1) * HEAD_DIM] = head(s).astype(_BF16)


def _in_proj(x, g, w, a_cols, head_gains, tables, seq, q_cols, k_cols, casts=()):
    m, d = x.shape
    n = w.shape[1]
    tm = _pick_tile(seq, (512, 256, 128))
    tn = next(t for t in (1024, 512, 256, 128) if a_cols % t == 0 and (n - a_cols) % t == 0)
    n_a = a_cols // tn

    def kind(col):
        return "q" if q_cols[0] <= col < q_cols[1] else "k" if k_cols[0] <= col < k_cols[1] else "p"

    kinds = tuple(tuple(kind(t * tn + s * HEAD_DIM) for s in range(tn // HEAD_DIM)) for t in range((n - a_cols) // tn))
    per_seq = seq // tm
    tab = pl.BlockSpec((tm, HEAD_DIM), lambda i, j: (i % per_seq, 0))
    grid = (m // tm, n // tn)
    cast_in, cast_out, cast_shapes = _cast_streams(casts, grid, 1)
    outs = pl.pallas_call(
        functools.partial(_in_proj_body, n_a=n_a, kinds=kinds, ncast=len(casts)),
        grid=grid,
        in_specs=[
            pl.BlockSpec((tm, d), lambda i, j: (i, 0)),
            pl.BlockSpec((1, d), lambda i, j: (0, 0)),
            pl.BlockSpec((d, tn), lambda i, j: (0, j)),
            pl.BlockSpec((2, HEAD_DIM), lambda i, j: (0, 0)),
            tab, tab, tab,
        ] + cast_in,
        out_specs=[
            pl.BlockSpec((tm, tn), lambda i, j: (i, jnp.minimum(j, n_a - 1))),
            pl.BlockSpec((tm, tn), lambda i, j: (i, jnp.maximum(j - n_a, 0))),
        ] + cast_out,
        out_shape=[
            jax.ShapeDtypeStruct((m, a_cols), _F32),
            jax.ShapeDtypeStruct((m, n - a_cols), _BF16),
        ] + cast_shapes,
        scratch_shapes=[pltpu.VMEM((tm, d), _BF16)],
        compiler_params=_params(("arbitrary", "arbitrary")),
        name="in_proj",
    )(x, g.reshape(1, d), w, head_gains, *tables, *[cw for cw, _ in casts])
    return outs[0], outs[1], outs[2:]


def _bucket_table():
    half = NUM_BUCKETS // 2
    max_exact = half // 2
    i = np.arange(Q_TILE_A)[:, None]
    c = np.arange(K_TILE_A)[None, :]
    out = np.zeros((len(A_BRANCHES), 3, Q_TILE_A, K_TILE_A), np.int32)
    for br, (_, dil) in enumerate(A_BRANCHES):
        for var in range(3):
            rel = c - SPAN * var - i
            dist = rel * dil
            n = np.abs(dist)
            t = np.log(np.maximum(n, 1) / max_exact) / math.log(MAX_DISTANCE / max_exact) * (half - max_exact)
            valid = np.abs(rel) <= SPAN
            frac = np.abs(t - np.round(t))
            assert np.all((frac > 1e-4) | (n <= max_exact) | (n >= MAX_DISTANCE) | ~valid), "bucket edge near an integer"
            large = np.minimum(max_exact + t.astype(np.int32), half - 1)
            bucket = np.where(dist > 0, half, 0) + np.where(n < max_exact, n, large)
            out[br, var] = np.where(valid, bucket, NUM_BUCKETS)
    return out


def _mixer_a_body(rel_ref, idx_ref, q_ref, k_ref, v_ref, o_ref, bias_ref, acc_ref, m_ref, l_ref, *, seq):
    h = pl.program_id(1)

    for br in range(len(A_BRANCHES)):
        for var in range(3):
            idx = idx_ref[br, var]

            def fill(bkt, bias, idx=idx):
                return jnp.where(idx == bkt, rel_ref[bkt, h], bias)

            bias_ref[br, var] = lax.fori_loop(0, NUM_BUCKETS, fill, jnp.full(idx.shape, NEG, _F32))

    order = sorted(range(len(A_BRANCHES)), key=lambda b: -A_BRANCHES[b][1])
    for br in order:
        dil = A_BRANCHES[br][1]
        length = seq // dil
        nblk = length // Q_TILE_A
        first = br == order[0]

        def rows(start, size, dil=dil):
            if dil == 1:
                return pl.ds(pl.multiple_of(start, SPAN), size)
            return pl.ds(start, size, stride=dil)

        def group(blocks, br=br, dil=dil, length=length, nblk=nblk, first=first, rows=rows):
            qsls, ss, vs = [], [], []
            whole = {}
            for n, r in blocks:
                p0 = n * Q_TILE_A
                qsl = rows(r + dil * p0, Q_TILE_A)
                q = (q_ref[qsl, :] * Q_SCALE).astype(_BF16)
                if isinstance(n, int):
                    ks = min(max(p0 - SPAN, 0), length - K_TILE_A)
                    var = 0 if n == 0 else 2 if n == nblk - 1 else 1
                    if id(r) not in whole:
                        seq_rows = rows(r, length)
                        whole[id(r)] = (k_ref[seq_rows, :].astype(_BF16), v_ref[seq_rows, :].astype(_BF16))
                    k, v = (t[ks:ks + K_TILE_A] for t in whole[id(r)])
                else:
                    ks = jnp.clip(p0 - SPAN, 0, length - K_TILE_A)
                    var = jnp.where(n == 0, 0, jnp.where(n == nblk - 1, 2, 1))
                    ksl = rows(r + dil * ks, K_TILE_A)
                    k, v = k_ref[ksl, :].astype(_BF16), v_ref[ksl, :].astype(_BF16)
                s = lax.dot_general(q, k, (((1,), (1,)), ((), ())), preferred_element_type=_F32)
                qsls.append(qsl)
                ss.append(s + bias_ref[br, var])
                vs.append(v)
            s = jnp.stack(ss)
            m_blk = jnp.max(s, axis=-1, keepdims=True)
            if first:
                m_new = m_blk
                p = jnp.exp(s - m_new)
            else:
                m_old = jnp.stack([m_ref[qsl, :] for qsl in qsls])
                m_new = jnp.maximum(m_old, m_blk)
                p = jnp.exp(s - jnp.concatenate([m_new] * (K_TILE_A // HEAD_DIM), axis=-1))
            l_blk = jnp.sum(p, axis=-1, keepdims=True)
            p = p.astype(_BF16)
            pv = jnp.stack([jnp.dot(p[j], vs[j], preferred_element_type=_F32) for j in range(GROUP_A)])
            if first:
                acc_new = pv
                l_new = jnp.broadcast_to(l_blk, pv.shape)
                m_new = jnp.broadcast_to(m_new, pv.shape)
            else:
                alpha = jnp.exp(m_old - m_new)
                acc_new = alpha * jnp.stack([acc_ref[qsl, :] for qsl in qsls]) + pv
                l_new = alpha * jnp.stack([l_ref[qsl, :] for qsl in qsls]) + l_blk
            for j, qsl in enumerate(qsls):
                acc_ref[qsl, :] = acc_new[j]
                l_ref[qsl, :] = l_new[j]
                m_ref[qsl, :] = m_new[j]

        if nblk <= GROUP_A:
            res_per_group = GROUP_A // nblk

            def body(rg, carry, group=group, res_per_group=res_per_group, nblk=nblk):
                residues = [rg * res_per_group + jr for jr in range(res_per_group)]
                group([(n, r) for r in residues for n in range(nblk)])
                return carry

            lax.fori_loop(0, dil // res_per_group, body, 0)
        else:
            res_per_group = min(dil, GROUP_A)
            blk_per_group = GROUP_A // res_per_group
            ngrp_n = nblk // blk_per_group

            def body(it, carry, group=group, res_per_group=res_per_group, blk_per_group=blk_per_group, ngrp_n=ngrp_n):
                rg, ng = it // ngrp_n, it % ngrp_n
                group([(ng * blk_per_group + jn, rg * res_per_group + jr)
                       for jn in range(blk_per_group) for jr in range(res_per_group)])
                return carry

            lax.fori_loop(0, (dil // res_per_group) * ngrp_n, body, 0)

    chunk = _pick_tile(seq, (512, 256, 128))

    def finish(c, carry):
        sl = pl.ds(pl.multiple_of(c * chunk, chunk), chunk)
        o_ref[sl, :] = (acc_ref[sl, :] / l_ref[sl, :]).astype(o_ref.dtype)
        return carry

    lax.fori_loop(0, seq // chunk, finish, 0)


def _mixer_a(proj_a, rel_bias, idx_tbl, batch, seq, heads):
    m = proj_a.shape[0]
    assert seq % (A_BRANCHES[-1][1] * K_TILE_A) == 0, "sequence too short for the widest dilation"
    for _, dil in A_BRANCHES:
        nblk = seq // (dil * Q_TILE_A)
        if nblk <= GROUP_A:
            assert GROUP_A % nblk == 0 and dil % (GROUP_A // nblk) == 0
        else:
            assert GROUP_A % min(dil, GROUP_A) == 0 and nblk % (GROUP_A // min(dil, GROUP_A)) == 0
    blk = (seq, HEAD_DIM)
    return pl.pallas_call(
        functools.partial(_mixer_a_body, seq=seq),
        grid=(batch, heads),
        in_specs=[
            pl.BlockSpec(memory_space=pltpu.SMEM),
            pl.BlockSpec(idx_tbl.shape, lambda b, h: (0, 0, 0, 0)),
            pl.BlockSpec(blk, lambda b, h: (b, h)),
            pl.BlockSpec(blk, lambda b, h: (b, heads + h)),
            pl.BlockSpec(blk, lambda b, h: (b, 2 * heads + h)),
        ],
        out_specs=pl.BlockSpec(blk, lambda b, h: (b, h)),
        out_shape=jax.ShapeDtypeStruct((m, heads * HEAD_DIM), _BF16),
        scratch_shapes=[
            pltpu.VMEM(idx_tbl.shape, _F32),
            pltpu.VMEM(blk, _F32),
            pltpu.VMEM(blk, _F32),
            pltpu.VMEM(blk, _F32),
        ],
        compiler_params=_params(("parallel", "parallel")),
        name="mixer_a",
    )(rel_bias, idx_tbl, proj_a, proj_a, proj_a)


def _mixer_b_body(gb_ref, gc_ref, hb_ref, gcp_ref, hbp_ref, gcn_ref, hbn_ref, w_ref, g_ref, o_ref, *, tiles_per_seq):
    i = pl.program_id(0)
    ts = gb_ref.shape[0]
    pos = i % tiles_per_seq
    u = gc_ref[...].astype(_F32) * hb_ref[...].astype(_F32)
    last = BF16_SUBLANE_TILE - 1
    u_prev = gcp_ref[last:last + 1, :].astype(_F32) * hbp_ref[last:last + 1, :].astype(_F32)
    u_next = gcn_ref[0:1, :].astype(_F32) * hbn_ref[0:1, :].astype(_F32)
    u_prev = jnp.where(pos == 0, 0.0, u_prev)
    u_next = jnp.where(pos == tiles_per_seq - 1, 0.0, u_next)
    row = lax.broadcasted_iota(jnp.int32, (ts, 1), 0)
    up = jnp.where(row == 0, u_prev, pltpu.roll(u, 1, 0))
    un = jnp.where(row == ts - 1, u_next, pltpu.roll(u, ts - 1, 0))
    y = gb_ref[...].astype(_F32) * (w_ref[0:1, :] * up + w_ref[1:2, :] * u + w_ref[2:3, :] * un)
    o_ref[...] = (y * _row_rms_inv(y) * g_ref[...]).astype(o_ref.dtype)


def _mixer_b(proj_r, conv_w, gain, seq):
    m = proj_r.shape[0]
    wb = gain.shape[0]
    ts = _pick_tile(seq, (512, 256, 128))
    halo = BF16_SUBLANE_TILE
    per = ts // halo
    nhalo = m // halo
    main = lambda c: pl.BlockSpec((ts, wb), lambda i: (i, c))
    prev = lambda c: pl.BlockSpec((halo, wb), lambda i: (jnp.maximum(i * per - 1, 0), c))
    nxt = lambda c: pl.BlockSpec((halo, wb), lambda i: (jnp.minimum((i + 1) * per, nhalo - 1), c))
    return pl.pallas_call(
        functools.partial(_mixer_b_body, tiles_per_seq=seq // ts),
        grid=(m // ts,),
        in_specs=[main(0), main(1), main(2), prev(1), prev(2), nxt(1), nxt(2),
                  pl.BlockSpec((CONV_WIDTH, wb), lambda i: (0, 0)),
                  pl.BlockSpec((1, wb), lambda i: (0, 0))],
        out_specs=pl.BlockSpec((ts, wb), lambda i: (i, 0)),
        out_shape=jax.ShapeDtypeStruct((m, wb), _BF16),
        compiler_params=_params(("parallel",)),
        name="mixer_b",
    )(proj_r, proj_r, proj_r, proj_r, proj_r, proj_r, proj_r, conv_w, gain.reshape(1, wb))


def _rope_tables(seq):
    quarter = HEAD_DIM // 4
    inv = ROPE_THETA ** (-jnp.arange(quarter, dtype=_F32) / quarter)
    t = jnp.arange(seq)
    ang_r = (t // GRID_W).astype(_F32)[:, None] * inv[None, :]
    ang_c = (t % GRID_W).astype(_F32)[:, None] * inv[None, :]
    zero = jnp.zeros_like(ang_r)
    cos = jnp.concatenate([jnp.cos(ang_r), jnp.cos(ang_r), jnp.cos(ang_c), jnp.cos(ang_c)], axis=-1)
    sin_hi = jnp.concatenate([-jnp.sin(ang_r), zero, -jnp.sin(ang_c), zero], axis=-1)
    sin_lo = jnp.concatenate([zero, jnp.sin(ang_r), zero, jnp.sin(ang_c)], axis=-1)
    return cos, sin_hi, sin_lo


def _attn_c_body(q_ref, k_ref, v_ref, *rest, tk, ncast):
    w_refs, o_ref, wo_refs = rest[:ncast], rest[ncast], rest[ncast + 1:2 * ncast + 1]
    vt_ref, qt_ref, acc_ref, *bufs = rest[2 * ncast + 1:]
    nchunk = vt_ref.shape[0]
    ntile = qt_ref.shape[0]
    tq = q_ref.shape[0] // ntile
    nq = GQA_GROUP * tq

    @pl.when(pl.program_id(2) == 0)
    def _():
        def transpose_v(c, carry):
            rows = pl.ds(pl.multiple_of(c * tk, tk), tk)
            vt_ref[c, 0:HEAD_DIM, :] = v_ref[rows, :].astype(_F32).T.astype(_BF16)
            vt_ref[c, HEAD_DIM:, :] = jnp.ones((BF16_SUBLANE_TILE, tk), _BF16)
            return carry

        lax.fori_loop(0, nchunk, transpose_v, 0)

    for t in range(ntile):
        for g in range(GQA_GROUP):
            q = q_ref[t * tq:(t + 1) * tq, g * HEAD_DIM:(g + 1) * HEAD_DIM]
            qt_ref[t, :, g * tq:(g + 1) * tq] = q.astype(_F32).T.astype(_BF16)
    acc_ref[...] = jnp.zeros(acc_ref.shape, _F32)
    for w_ref, wo_ref in zip(w_refs, wo_refs):
        wo_ref[...] = w_ref[...].astype(wo_ref.dtype)

    s_sets = (bufs[0:2], bufs[2:4])
    p_sets = (bufs[4:6], bufs[6:8])
    npair = nchunk // 2
    nitem = ntile * npair

    def stage_scores(item, s_set):
        tile, pair = item // npair, item % npair
        out = []
        for j in range(2):
            rows = pl.ds(pl.multiple_of((2 * pair + j) * tk, tk), tk)
            s = jnp.dot(k_ref[rows, :], qt_ref[tile], preferred_element_type=_F32)
            s_set[j][...] = s
            out.append(jnp.max(s, axis=0, keepdims=True))
        return tuple(out)

    def stage_exp(item, s_set, p_set, maxes, m_run):
        m_run = jnp.where(item % npair == 0, NEG, m_run)
        alphas = []
        for j in range(2):
            m_new = jnp.maximum(m_run, maxes[j])
            alphas.append(jnp.exp2(m_run - m_new))
            p_set[j][...] = jnp.exp2((s_set[j][...] - m_new).astype(_BF16))
            m_run = m_new
        return tuple(alphas), m_run

    def stage_pv(item, p_set, alphas):
        tile, pair = item // npair, item % npair
        for j in range(2):
            acc_ref[tile] = alphas[j] * acc_ref[tile] + jnp.dot(vt_ref[2 * pair + j], p_set[j][...],
                                                                 preferred_element_type=_F32)

    m_run = jnp.full((1, nq), NEG, _F32)
    mx0 = stage_scores(0, s_sets[0])
    mx1 = stage_scores(1, s_sets[1])
    al0, m_run = stage_exp(0, s_sets[0], p_sets[0], mx0, m_run)

    def two_steps(it, carry):
        m_run, mx1, al0 = carry
        j = 2 * it
        mx0 = stage_scores(j + 2, s_sets[0])
        al1, m_run = stage_exp(j + 1, s_sets[1], p_sets[1], mx1, m_run)
        stage_pv(j, p_sets[0], al0)
        mx1 = stage_scores(j + 3, s_sets[1])
        al0, m_run = stage_exp(j + 2, s_sets[0], p_sets[0], mx0, m_run)
        stage_pv(j + 1, p_sets[1], al1)
        return m_run, mx1, al0

    m_run, mx1, al0 = lax.fori_loop(0, nitem // 2 - 1, two_steps, (m_run, mx1, al0))
    al1, m_run = stage_exp(nitem - 1, s_sets[1], p_sets[1], mx1, m_run)
    stage_pv(nitem - 2, p_sets[0], al0)
    stage_pv(nitem - 1, p_sets[1], al1)
    for t in range(ntile):
        out_t = acc_ref[t, 0:HEAD_DIM, :] / acc_ref[t, HEAD_DIM:HEAD_DIM + 1, :]
        for g in range(GQA_GROUP):
            o_ref[t * tq:(t + 1) * tq, g * HEAD_DIM:(g + 1) * HEAD_DIM] = (
                out_t[:, g * tq:(g + 1) * tq].T.astype(o_ref.dtype))


def _attn_c(proj_r, batch, seq, q_heads, kv_heads, q_col0, casts):
    m = proj_r.shape[0]
    assert q_heads == GQA_GROUP * kv_heads and q_col0 % (GQA_GROUP * HEAD_DIM) == 0
    q_blk0 = q_col0 // (GQA_GROUP * HEAD_DIM)
    k_blk0 = q_col0 // HEAD_DIM + q_heads
    v_blk0 = k_blk0 + kv_heads
    tq = _pick_tile(seq, (256, 128))
    ntile = next(t for t in (4, 2, 1) if seq % (t * tq) == 0)
    tk = _pick_tile(seq, (512, 256, 128))
    assert (seq // tk) % 4 == 0, "the chunk pipeline advances two pairs of key chunks per loop step"
    tq_step = ntile * tq
    nq = seq // tq_step
    gw = GQA_GROUP * HEAD_DIM
    nsteps = batch * kv_heads * nq
    step = lambda b, g, i: (b * kv_heads + g) * nq + i
    cast_in, cast_out, cast_shapes = [], [], []
    for w, layer in casts:
        _, k, n = w.shape
        assert k % (nsteps * BF16_SUBLANE_TILE) == 0
        rows = k // nsteps
        cast_in.append(pl.BlockSpec((None, rows, n), lambda b, g, i, layer=layer: (layer, step(b, g, i), 0)))
        cast_out.append(pl.BlockSpec((rows, n), lambda b, g, i: (step(b, g, i), 0)))
        cast_shapes.append(jax.ShapeDtypeStruct((k, n), _BF16))
    outs = pl.pallas_call(
        functools.partial(_attn_c_body, tk=tk, ncast=len(casts)),
        grid=(batch, kv_heads, nq),
        in_specs=[pl.BlockSpec((tq_step, gw), lambda b, g, i: (b * nq + i, q_blk0 + g)),
                  pl.BlockSpec((seq, HEAD_DIM), lambda b, g, i: (b, k_blk0 + g)),
                  pl.BlockSpec((seq, HEAD_DIM), lambda b, g, i: (b, v_blk0 + g))] + cast_in,
        out_specs=[pl.BlockSpec((tq_step, gw), lambda b, g, i: (b * nq + i, g))] + cast_out,
        out_shape=[jax.ShapeDtypeStruct((m, q_heads * HEAD_DIM), _BF16)] + cast_shapes,
        scratch_shapes=[pltpu.VMEM((seq // tk, HEAD_DIM + BF16_SUBLANE_TILE, tk), _BF16),
                        pltpu.VMEM((ntile, HEAD_DIM, GQA_GROUP * tq), _BF16),
                        pltpu.VMEM((ntile, HEAD_DIM + BF16_SUBLANE_TILE, GQA_GROUP * tq), _F32)]
        + [pltpu.VMEM((tk, GQA_GROUP * tq), _F32)] * 4
        + [pltpu.VMEM((tk, GQA_GROUP * tq), _BF16)] * 4,
        compiler_params=_params(("arbitrary", "arbitrary", "arbitrary")),
        name="attn_c",
    )(proj_r, proj_r, proj_r, *[w for w, _ in casts])
    return outs[0], outs[1:]


def _out_proj_body(oa_ref, ob_ref, oc_ref, ga_ref, gc_ref, w_ref, x_ref, gp_ref, o_ref, xn_ref, *, tk):
    k = pl.program_id(1)
    nk = pl.num_programs(1)

    @pl.when(k == 0)
    def _():
        def fn(rows):
            c0 = 0
            for ref, g_ref in ((oa_ref, ga_ref), (ob_ref, None), (oc_ref, gc_ref)):
                inv = None if g_ref is None else _rows_rms_inv(ref, rows)
                for cols in _lane_tiles(ref.shape[1]):
                    val = ref[rows, cols]
                    if g_ref is not None:
                        val = (val.astype(_F32) * inv * g_ref[:, cols]).astype(_BF16)
                    xn_ref[c0 + cols.start // tk, rows, cols.start % tk:cols.start % tk + LANES] = val
                c0 += ref.shape[1] // tk

        _for_row_chunks(oa_ref.shape[0], fn)
        o_ref[...] = jnp.dot(xn_ref[0], w_ref[...], preferred_element_type=_F32)

    @pl.when(k > 0)
    def _():
        o_ref[...] += jnp.dot(xn_ref[k], w_ref[...], preferred_element_type=_F32)

    @pl.when(k == nk - 1)
    def _():
        _residual_norm_rows(o_ref, x_ref, gp_ref)


def _out_proj(oa, ob, oc, ga, gc, w, x, gp):
    m, d = x.shape
    kdim = w.shape[0]
    wa, wb, wc = oa.shape[1], ob.shape[1], oc.shape[1]
    tm = _pick_tile(m, (512, 256, 128))
    tk = next(t for t in (512, 256, 128) if wa % t == 0 and wb % t == 0 and wc % t == 0)
    nk = kdim // tk
    row = lambda width: pl.BlockSpec((tm, width), lambda i, k: (i, 0))
    vec = lambda width: pl.BlockSpec((1, width), lambda i, k: (0, 0))
    return pl.pallas_call(
        functools.partial(_out_proj_body, tk=tk),
        grid=(m // tm, nk),
        in_specs=[row(wa), row(wb), row(wc), vec(wa), vec(wc),
                  pl.BlockSpec((tk, d), lambda i, k: (k, 0)),
                  row(d), vec(d)],
        out_specs=row(d),
        out_shape=jax.ShapeDtypeStruct((m, d), _F32),
        scratch_shapes=[pltpu.VMEM((nk, tm, tk), _BF16)],
        compiler_params=_params(("parallel", "arbitrary")),
        name="out_proj",
    )(oa, ob, oc, ga.reshape(1, wa), gc.reshape(1, wc), w, x, gp.reshape(1, d))


def _mlp_body(x_ref, g1_ref, wu_ref, wd_ref, g2_ref, *rest, nf, ncast):
    o_ref, (xn_ref, *hid_refs) = rest[ncast], rest[2 * ncast + 1:]
    cast_slabs = functools.partial(_cast_slabs, rest[:ncast], rest[ncast + 1:2 * ncast + 1])
    f = pl.program_id(1)

    def up(dst_ref):
        hid = jnp.dot(xn_ref[...], wu_ref[...], preferred_element_type=_F32)
        dst_ref[...] = jnp.square(jnp.maximum(hid, 0.0)).astype(_BF16)

    def down(src_ref):
        o_ref[...] += jnp.dot(src_ref[...], wd_ref[...], preferred_element_type=_F32)

    @pl.when(f == 0)
    def _():
        _norm_rows_to(xn_ref, x_ref, g1_ref)
        o_ref[...] = jnp.zeros(o_ref.shape, _F32)
        cast_slabs()
        up(hid_refs[0])

    for parity in range(2):
        @pl.when((f > 0) & (f < nf) & (f % 2 == parity))
        def _(parity=parity):
            cast_slabs()
            up(hid_refs[parity])
            down(hid_refs[1 - parity])

    @pl.when(f == nf)
    def _():
        cast_slabs()
        down(hid_refs[(nf - 1) % 2])
        _residual_norm_rows(o_ref, x_ref, g2_ref)


MLP_CAST_COLUMN_SPLIT = 2


def _mlp(x, g1, wu, wd, g2, casts=()):
    m, d = x.shape
    dff = wu.shape[1]
    tm = _pick_tile(m, (512, 256, 128))
    tf = _pick_tile(dff, (512, 256, 128))
    row = pl.BlockSpec((tm, d), lambda i, f: (i, 0))
    vec = pl.BlockSpec((1, d), lambda i, f: (0, 0))
    nf = dff // tf
    grid = (m // tm, nf + 1)
    cast_in, cast_out, cast_shapes = _cast_streams(casts, grid, MLP_CAST_COLUMN_SPLIT)
    outs = pl.pallas_call(
        functools.partial(_mlp_body, nf=nf, ncast=len(casts)),
        grid=grid,
        in_specs=[row, vec,
                  pl.BlockSpec((d, tf), lambda i, f: (0, jnp.minimum(f, nf - 1))),
                  pl.BlockSpec((tf, d), lambda i, f: (jnp.maximum(f - 1, 0), 0)),
                  vec] + cast_in,
        out_specs=[row] + cast_out,
        out_shape=[jax.ShapeDtypeStruct((m, d), _F32)] + cast_shapes,
        scratch_shapes=[pltpu.VMEM((tm, d), _BF16), pltpu.VMEM((tm, tf), _BF16), pltpu.VMEM((tm, tf), _BF16)],
        compiler_params=_params(("arbitrary", "arbitrary")),
        name="mlp",
    )(x, g1.reshape(1, d), wu, wd, g2.reshape(1, d), *[cw for cw, _ in casts])
    return outs[0], outs[1:]


def kernel(x, rel_bias, pre_mix_norm, w_in, conv_w, q_norm, k_norm, out_norm_a, out_norm_b, out_norm_c,
           w_out, post_mix_norm, pre_mlp_norm, w_up, w_down, post_mlp_norm):
    batch, seq, d = x.shape
    depth = w_in.shape[0]
    wa, wb, wc = out_norm_a.shape[1], out_norm_b.shape[1], out_norm_c.shape[1]
    in_width = w_in.shape[2]
    kv_width = (in_width - 3 * wa - 3 * wb - wc) // 2
    a_heads, q_heads, kv_heads = wa // HEAD_DIM, wc // HEAD_DIM, kv_width // HEAD_DIM
    assert all(w // (2 * dil) == SPAN for w, dil in A_BRANCHES)
    assert rel_bias.shape == (NUM_BUCKETS, a_heads) and seq % GRID_W == 0

    idx_tbl = jnp.asarray(_bucket_table())
    tables = _rope_tables(seq)
    xf = x.reshape(batch * seq, d)
    w_in_b = _cast_bf16(w_in, 0)
    for i in range(depth):
        q0 = 3 * wb
        casts = [(w_out, 0), (w_up, 0)] if i == 0 else []
        proj_a, proj_r, cast = _in_proj(xf, pre_mix_norm[i], w_in_b, 3 * wa, jnp.stack([q_norm[i], k_norm[i]]), tables,
                                        seq, (q0, q0 + wc), (q0 + wc, q0 + wc + kv_width), casts)
        if i == 0:
            w_out_b, w_up_b = cast
        oa = _mixer_a(proj_a, rel_bias, idx_tbl, batch, seq, a_heads)
        ob = _mixer_b(proj_r, conv_w[i], out_norm_b[i], seq)
        oc, (w_down_b,) = _attn_c(proj_r, batch, seq, q_heads, kv_heads, q0, [(w_down, i)])
        xf = _out_proj(oa, ob, oc, out_norm_a[i], out_norm_c[i], w_out_b, xf, post_mix_norm[i])
        casts = [(w, i + 1) for w in (w_in, w_out, w_up)] if i + 1 < depth else []
        xf, cast = _mlp(xf, pre_mlp_norm[i], w_up_b, w_down_b, post_mlp_norm[i], casts)
        if casts:
            w_in_b, w_out_b, w_up_b = cast
    return xf.reshape(batch, seq, d)
```
